```python
import math
import jax
import jax.numpy as jnp
from jax import lax
import numpy as np

D_MODEL = 1024
BATCH = 8
SEQ = 8192
DEPTH = 2

CTX_LEN = 256
GRID_W = 64
D_MIX = D_MODEL
D_S5 = D_MIX // 2
D_POOL = D_MIX - D_S5
S5_H = 16
S5_G = D_S5 // S5_H
S5_P = 64
N_DIRS = 2
POOL_WINDOWS = (2, 4, 8, 16)
POOL_CG = D_POOL // len(POOL_WINDOWS)
N_EXPERTS = 16
EC_CAPACITY_FACTOR = 2
EXPERT_FF = D_MODEL
N_MOD = 6
RMS_EPS = 1e-6

kernel_name = 'hybrid_s5_pool_ecmoe_prefix_dit'


def rmsnorm(x, g):
    xf = x.astype(jnp.float32)
    y = xf * lax.rsqrt(jnp.mean(xf * xf, axis=-1, keepdims=True) + RMS_EPS)
    return (y * g.astype(jnp.float32)).astype(x.dtype)


def modulate(h, shift, scale):
    return h * (1 + scale) + shift


def s5_discretize(lam_re, lam_im, log_dt, b_re, b_im):
    lam = lax.complex(lam_re.astype(jnp.float32), lam_im.astype(jnp.float32))
    dt = jnp.exp(log_dt.astype(jnp.float32))[:, None]
    a_bar = jnp.exp(lam * dt)
    b_mat = lax.complex(b_re.astype(jnp.float32), b_im.astype(jnp.float32))
    b_bar = ((a_bar - 1) / lam)[:, :, None] * b_mat
    return a_bar, b_bar


def _ssm_combine(left, right):
    a1, b1 = left
    a2, b2 = right
    return a1 * a2, a2 * b1 + b2


def s5_scan(u, a_bar, b_bar, h0, reverse):
    bu = jnp.einsum('btgh,gph->btgp', u.astype(jnp.complex64), b_bar)
    if h0 is not None:
        edge = -1 if reverse else 0
        bu = bu.at[:, edge].add(a_bar[None] * h0)
    a = jnp.broadcast_to(a_bar[None, None], (1, u.shape[1]) + a_bar.shape)
    _, h = lax.associative_scan(_ssm_combine, (a, bu), reverse=reverse, axis=1)
    return h


def s5_glu(y, w_glu):
    y = jax.nn.gelu(y)
    return y * jax.nn.sigmoid(y @ w_glu.astype(jnp.float32))


def s5_mixer(u_x, u_c, lam_re, lam_im, log_dt, b_re, b_im, c_re, c_im, d_skip, w_glu, with_ctx):
    bn, t_len, _ = u_x.shape
    l_len = u_c.shape[1]
    ux = u_x.astype(jnp.float32).reshape(bn, t_len, S5_G, S5_H)
    uc = u_c.astype(jnp.float32).reshape(bn, l_len, S5_G, S5_H)
    d = d_skip.astype(jnp.float32).reshape(S5_G, S5_H)
    y_x = d * ux
    y_c = d * uc if with_ctx else None
    for direction in range(N_DIRS):
        reverse = direction == 1
        a_bar, b_bar = s5_discretize(lam_re[direction], lam_im[direction], log_dt[direction],
                                     b_re[direction], b_im[direction])
        cmat = lax.complex(c_re[direction].astype(jnp.float32), c_im[direction].astype(jnp.float32))
        h_c = s5_scan(uc, a_bar, b_bar, None, reverse)
        h0 = h_c[:, 0] if reverse else h_c[:, -1]
        h_x = s5_scan(ux, a_bar, b_bar, h0, reverse)
        y_x = y_x + jnp.einsum('ghp,btgp->btgh', cmat, h_x).real
        if with_ctx:
            y_c = y_c + jnp.einsum('ghp,btgp->btgh', cmat, h_c).real
    out_x = s5_glu(y_x.reshape(bn, t_len, D_S5), w_glu).astype(u_x.dtype)
    out_c = s5_glu(y_c.reshape(bn, l_len, D_S5), w_glu).astype(u_c.dtype) if with_ctx else None
    return out_x, out_c


def box_mean(v, w, axis):
    n = v.shape[axis]
    cs = jnp.cumsum(v, axis=axis)
    pad = [(0, 0)] * v.ndim
    pad[axis] = (1, 0)
    cs = jnp.pad(cs, pad)
    t = jnp.arange(n)
    lo = jnp.clip(t - w // 2, 0, n)
    hi = jnp.clip(t + w // 2, 0, n)
    shape = [1] * v.ndim
    shape[axis] = n
    cnt = (hi - lo).astype(v.dtype).reshape(shape)
    return (jnp.take(cs, hi, axis=axis) - jnp.take(cs, lo, axis=axis)) / cnt


def pool_mixer(v, pool_w, pool_scale, grid):
    bn, t_len, _ = v.shape
    vf = v.astype(jnp.float32)
    outs = []
    for i, w in enumerate(POOL_WINDOWS):
        vg = vf[..., i * POOL_CG:(i + 1) * POOL_CG]
        if grid:
            rows = t_len // GRID_W
            v4 = vg.reshape(bn, rows, GRID_W, POOL_CG)
            m = box_mean(box_mean(v4, w, 1), w, 2).reshape(bn, t_len, POOL_CG)
        else:
            m = box_mean(vg, w, 1)
        outs.append(m - vg)
    z = jnp.stack(outs, axis=2)
    z = jnp.einsum('btgc,gcd->btgd', z, pool_w.astype(jnp.float32)).reshape(bn, t_len, D_POOL)
    return (z * pool_scale.astype(jnp.float32)).astype(v.dtype)


def ec_moe(h, w_router, w_gate, w_up, w_down):
    bn, t_len, d = h.shape
    cap = EC_CAPACITY_FACTOR * t_len // N_EXPERTS
    logits = jnp.einsum('btd,de->bte', h.astype(jnp.float32), w_router.astype(jnp.float32))
    affinity = jax.nn.softmax(logits, axis=-1)
    gate, idx = lax.top_k(jnp.swapaxes(affinity, 1, 2), cap)
    xs = jax.vmap(lambda hb, ib: hb[ib])(h, idx)
    a = jnp.einsum('becd,edf->becf', xs, w_gate)
    g = jnp.einsum('becd,edf->becf', xs, w_up)
    y = jnp.einsum('becf,efd->becd', jax.nn.silu(a) * g, w_down)
    y = y * gate[..., None].astype(y.dtype)
    out = jax.vmap(lambda yb, ib: jnp.zeros((t_len, d), yb.dtype).at[ib.reshape(-1)].add(yb.reshape(-1, d)))(y, idx)
    return out


def setup_inputs(seed: int = 0) -> dict:
    key = jax.random.key(seed)
    ks = jax.random.split(key, 32)

    def nrm(k, shape, scale):
        return jax.random.normal(k, shape, jnp.float32) * scale

    lam_im_base = jnp.broadcast_to(math.pi * jnp.arange(S5_P, dtype=jnp.float32), (DEPTH, N_DIRS, S5_G, S5_P))
    return {
        'x': nrm(ks[0], (BATCH, SEQ, D_MODEL), 1.0),
        'c': nrm(ks[1], (BATCH, D_MODEL), 1.0),
        'ctx': nrm(ks[2], (BATCH, CTX_LEN, D_MODEL), 1.0),
        'c_ctx': nrm(ks[3], (D_MODEL,), 1.0),
        'w_mod': nrm(ks[4], (DEPTH, D_MODEL, N_MOD * D_MODEL), 0.5 * D_MODEL ** -0.5),
        'b_mod': nrm(ks[5], (DEPTH, N_MOD * D_MODEL), 0.02),
        'g_mix_pre': 1.0 + nrm(ks[6], (DEPTH, D_MODEL), 0.02),
        'g_mix_post': 1.0 + nrm(ks[7], (DEPTH, D_MODEL), 0.02),
        'g_ffn_pre': 1.0 + nrm(ks[8], (DEPTH, D_MODEL), 0.02),
        'g_ffn_post': 1.0 + nrm(ks[9], (DEPTH, D_MODEL), 0.02),
        'w_in': nrm(ks[10], (DEPTH, D_MODEL, D_MIX), D_MODEL ** -0.5),
        's5_lam_re': -0.5 + nrm(ks[11], (DEPTH, N_DIRS, S5_G, S5_P), 0.01),
        's5_lam_im': lam_im_base + nrm(ks[12], (DEPTH, N_DIRS, S5_G, S5_P), 0.01),
        's5_log_dt': jax.random.uniform(ks[13], (DEPTH, N_DIRS, S5_G), jnp.float32, math.log(1e-3), math.log(1e-1)),
        's5_b_re': nrm(ks[14], (DEPTH, N_DIRS, S5_G, S5_P, S5_H), (2.0 * S5_H) ** -0.5),
        's5_b_im': nrm(ks[15], (DEPTH, N_DIRS, S5_G, S5_P, S5_H), (2.0 * S5_H) ** -0.5),
        's5_c_re': nrm(ks[16], (DEPTH, N_DIRS, S5_G, S5_H, S5_P), (2.0 * S5_P) ** -0.5),
        's5_c_im': nrm(ks[17], (DEPTH, N_DIRS, S5_G, S5_H, S5_P), (2.0 * S5_P) ** -0.5),
        's5_d': nrm(ks[18], (DEPTH, D_S5), 1.0),
        's5_w_glu': nrm(ks[19], (DEPTH, D_S5, D_S5), D_S5 ** -0.5),
        'pool_w': nrm(ks[20], (DEPTH, len(POOL_WINDOWS), POOL_CG, POOL_CG), POOL_CG ** -0.5),
        'pool_scale': 1.0 + nrm(ks[21], (DEPTH, D_POOL), 0.1),
        'w_out': nrm(ks[22], (DEPTH, D_MIX, D_MODEL), D_MIX ** -0.5),
        'w_router': nrm(ks[23], (DEPTH, D_MODEL, N_EXPERTS), D_MODEL ** -0.5),
        'w_gate': nrm(ks[24], (DEPTH, N_EXPERTS, D_MODEL, EXPERT_FF), D_MODEL ** -0.5),
        'w_up': nrm(ks[25], (DEPTH, N_EXPERTS, D_MODEL, EXPERT_FF), D_MODEL ** -0.5),
        'w_down': nrm(ks[26], (DEPTH, N_EXPERTS, EXPERT_FF, D_MODEL), EXPERT_FF ** -0.5),
    }


def reference(x, c, ctx, c_ctx, w_mod, b_mod, g_mix_pre, g_mix_post, g_ffn_pre, g_ffn_post,
              w_in, s5_lam_re, s5_lam_im, s5_log_dt, s5_b_re, s5_b_im, s5_c_re, s5_c_im,
              s5_d, s5_w_glu, pool_w, pool_scale, w_out, w_router, w_gate, w_up, w_down):
    for l in range(DEPTH):
        last = l == DEPTH - 1
        wm = w_mod[l].astype(jnp.float32)
        bm = b_mod[l].astype(jnp.float32)
        mod_x = (jax.nn.silu(c.astype(jnp.float32)) @ wm + bm).astype(x.dtype)[:, None, :]
        mod_c = (jax.nn.silu(c_ctx.astype(jnp.float32)) @ wm + bm).astype(ctx.dtype)
        shx1, scx1, gtx1, shx2, scx2, gtx2 = jnp.split(mod_x, N_MOD, axis=-1)
        shc1, scc1, gtc1, shc2, scc2, gtc2 = jnp.split(mod_c, N_MOD, axis=-1)

        hx = modulate(rmsnorm(x, g_mix_pre[l]), shx1, scx1)
        hc = modulate(rmsnorm(ctx, g_mix_pre[l]), shc1, scc1)
        px = hx @ w_in[l]
        pc = hc @ w_in[l]
        s5_x, s5_c = s5_mixer(px[..., :D_S5], pc[..., :D_S5], s5_lam_re[l], s5_lam_im[l], s5_log_dt[l],
                              s5_b_re[l], s5_b_im[l], s5_c_re[l], s5_c_im[l], s5_d[l], s5_w_glu[l],
                              not last)
        pool_x = pool_mixer(px[..., D_S5:], pool_w[l], pool_scale[l], True)
        mix_x = jnp.concatenate([s5_x, pool_x], axis=-1) @ w_out[l]
        x = x + gtx1 * rmsnorm(mix_x, g_mix_post[l])
        if not last:
            pool_c = pool_mixer(pc[..., D_S5:], pool_w[l], pool_scale[l], False)
            mix_c = jnp.concatenate([s5_c, pool_c], axis=-1) @ w_out[l]
            ctx = ctx + gtc1 * rmsnorm(mix_c, g_mix_post[l])

        hx = modulate(rmsnorm(x, g_ffn_pre[l]), shx2, scx2)
        x = x + gtx2 * rmsnorm(ec_moe(hx, w_router[l], w_gate[l], w_up[l], w_down[l]), g_ffn_post[l])
        if not last:
            hc = modulate(rmsnorm(ctx, g_ffn_pre[l]), shc2, scc2)
            ctx = ctx + gtc2 * rmsnorm(ec_moe(hc, w_router[l], w_gate[l], w_up[l], w_down[l]), g_ffn_post[l])
    return x
```

```python
import functools
import math

import numpy as np
import jax
import jax.numpy as jnp
from jax import lax
from jax.experimental import pallas as pl
from jax.experimental.pallas import tpu as pltpu

GRID_W = 64
POOL_WINDOWS = (2, 4, 8, 16)
EC_CAPACITY_FACTOR = 2
RMS_EPS = 1e-6

LANES = 128
SUBLANES = 8
BF16_ROWS = 16
VMEM_LIMIT = 56 * 1024 * 1024

F32 = jnp.float32
BF16 = jnp.bfloat16
HI = lax.Precision.HIGHEST


def _cparams(sem):
    return pltpu.CompilerParams(dimension_semantics=sem, vmem_limit_bytes=VMEM_LIMIT)


def _rms(v, g):
    return v * lax.rsqrt(jnp.mean(v * v, axis=-1, keepdims=True) + RMS_EPS) * g


def _mod_body(c_ref, w_ref, b_ref, o_ref):
    s = jax.nn.silu(c_ref[...])
    o_ref[...] = jnp.dot(s, w_ref[...], preferred_element_type=F32, precision=HI) + b_ref[...]


def mod_call(cc, wm, bm):
    r, d = cc.shape
    n = wm.shape[1]
    return pl.pallas_call(
        _mod_body,
        grid=(n // d,),
        in_specs=[pl.BlockSpec((r, d), lambda j: (0, 0)),
                  pl.BlockSpec((d, d), lambda j: (0, j)),
                  pl.BlockSpec((1, d), lambda j: (0, j))],
        out_specs=pl.BlockSpec((r, d), lambda j: (0, j)),
        out_shape=jax.ShapeDtypeStruct((r, n), F32),
        compiler_params=_cparams(("arbitrary",)),
        name="mod",
    )(cc, wm, bm.reshape(1, n))


def _mixer_in_body(x_ref, mod_ref, g_ref, ws_ref, wp_ref, pxt_ref, pxp_ref):
    m = mod_ref[0]
    h = _rms(x_ref[0], g_ref[...]) * (1.0 + m[1:2]) + m[0:1]
    hb = h.astype(BF16)
    pxp_ref[0] = jnp.dot(hb, wp_ref[...], preferred_element_type=F32).astype(BF16)
    pt = lax.dot_general(ws_ref[...], hb, (((1,), (1,)), ((), ())), preferred_element_type=F32)
    for j in range(pxt_ref.shape[0]):
        pxt_ref[j] = pt[:, LANES * j:LANES * (j + 1)]


def mixer_in_call(x, mod3, mod_row, g, ws_t, wp):
    b, t, d = x.shape
    ds, dp = ws_t.shape[0], wp.shape[1]
    tm = min(512, t)
    nj = tm // LANES
    mod_idx = (lambda i, j: (i, 0, 0)) if mod_row is None else (lambda i, j: (mod_row, 0, 0))
    return pl.pallas_call(
        _mixer_in_body,
        grid=(b, t // tm),
        in_specs=[pl.BlockSpec((1, tm, d), lambda i, j: (i, j, 0)),
                  pl.BlockSpec((1, 6, d), mod_idx),
                  pl.BlockSpec((1, d), lambda i, j: (0, 0)),
                  pl.BlockSpec((ds, d), lambda i, j: (0, 0)),
                  pl.BlockSpec((d, dp), lambda i, j: (0, 0))],
        out_specs=[pl.BlockSpec((nj, None, ds, LANES), lambda i, j: (j, i, 0, 0)),
                   pl.BlockSpec((1, tm, dp), lambda i, j: (i, j, 0))],
        out_shape=[jax.ShapeDtypeStruct((t // LANES, b, ds, LANES), F32),
                   jax.ShapeDtypeStruct((b, t, dp), BF16)],
        compiler_params=_cparams(("arbitrary", "arbitrary")),
        name="mixer_in",
    )(x, mod3, g.reshape(1, d), ws_t, wp)


def s5_tables(lam_re, lam_im, log_dt, b_re, b_im, c_re, c_im, d_skip):
    n_dir, g, p = lam_re.shape
    h = b_re.shape[-1]
    lc = LANES
    lam_re, lam_im = lam_re.astype(F32), lam_im.astype(F32)
    dt = jnp.exp(log_dt.astype(F32))[..., None]
    zr, zi = lam_re * dt, lam_im * dt
    er = jnp.exp(zr)
    ar, ai = er * jnp.cos(zi), er * jnp.sin(zi)
    den = lam_re * lam_re + lam_im * lam_im
    qr = ((ar - 1.0) * lam_re + ai * lam_im) / den
    qi = (ai * lam_re - (ar - 1.0) * lam_im) / den
    br, bi = b_re.astype(F32), b_im.astype(F32)
    bbr = qr[..., None] * br - qi[..., None] * bi
    bbi = qr[..., None] * bi + qi[..., None] * br
    cr, ci = c_re.astype(F32), c_im.astype(F32)
    jj = jnp.arange(lc + 1, dtype=F32)
    pe = jnp.exp(zr[..., None] * jj)
    pr, pi = pe * jnp.cos(zi[..., None] * jj), pe * jnp.sin(zi[..., None] * jj)
    e_r = jnp.einsum('dgop,dgpi->dgpoi', cr, bbr, precision=HI) - jnp.einsum('dgop,dgpi->dgpoi', ci, bbi, precision=HI)
    e_i = jnp.einsum('dgop,dgpi->dgpoi', cr, bbi, precision=HI) + jnp.einsum('dgop,dgpi->dgpoi', ci, bbr, precision=HI)
    kk = (jnp.einsum('dgpj,dgpoi->dgjoi', pr[..., :lc], e_r, precision=HI)
          - jnp.einsum('dgpj,dgpoi->dgjoi', pi[..., :lc], e_i, precision=HI))
    kf = jnp.transpose(kk[0], (0, 3, 2, 1))
    kr = jnp.transpose(kk[1], (0, 3, 2, 1))
    dsk = d_skip.astype(F32).reshape(g, h)
    eye = jnp.eye(h, dtype=F32)[None] * dsk[:, :, None]
    w0 = kf[..., 0] + kr[..., 0] + eye
    wtoe = jnp.concatenate([w0[..., None], kf[..., 1:], jnp.zeros_like(w0)[..., None], kr[..., :0:-1]], axis=-1)
    wtoe = wtoe.reshape(g, h * h, 2 * lc)
    fpr, fpi = pr[0][..., lc - 1::-1][..., :lc], pi[0][..., lc - 1::-1][..., :lc]
    rpr, rpi = pr[1][..., :lc], pi[1][..., :lc]
    def bproj(xr, xi, d):
        re = jnp.einsum('gpt,gph->ghtp', xr, bbr[d]) - jnp.einsum('gpt,gph->ghtp', xi, bbi[d])
        im = jnp.einsum('gpt,gph->ghtp', xr, bbi[d]) + jnp.einsum('gpt,gph->ghtp', xi, bbr[d])
        return re, im
    fre, fim = bproj(fpr, fpi, 0)
    rre, rim = bproj(rpr, rpi, 1)
    wb = jnp.concatenate([fre, fim, rre, rim], axis=-1).reshape(g, h * lc, 4 * p).astype(BF16)
    ipr_f, ipi_f = pr[0][..., 1:lc + 1], pi[0][..., 1:lc + 1]
    ipr_r, ipi_r = pr[1][..., lc:0:-1], pi[1][..., lc:0:-1]
    def cproj(xr, xi, d):
        re = jnp.einsum('gop,gpt->gpot', cr[d], xr) - jnp.einsum('gop,gpt->gpot', ci[d], xi)
        im = jnp.einsum('gop,gpt->gpot', cr[d], xi) + jnp.einsum('gop,gpt->gpot', ci[d], xr)
        return re, -im
    c_fr, c_fi = cproj(ipr_f, ipi_f, 0)
    c_rr, c_ri = cproj(ipr_r, ipi_r, 1)
    wc = jnp.concatenate([c_fr, c_fi, c_rr, c_ri], axis=1).reshape(g, 4 * p, h * lc).astype(BF16)
    dre = jnp.stack([pr[0][..., lc], pr[0][..., lc], pr[1][..., lc], pr[1][..., lc]], axis=1)
    dim_ = jnp.stack([-pi[0][..., lc], pi[0][..., lc], -pi[1][..., lc], pi[1][..., lc]], axis=1)
    return wtoe, wb, wc, dre.reshape(1, g * 4 * p), dim_.reshape(1, g * 4 * p)


def _gather_chunk_rows(x_ref, vbuf, r0, rows):
    nh = x_ref.shape[1]
    for c0 in range(0, rows, SUBLANES):
        for h in range(nh):
            vbuf[c0:c0 + SUBLANES, LANES * h:LANES * (h + 1)] = x_ref[pl.ds(r0 + c0, SUBLANES), h, :]


def _scatter_chunk_rows(y, o_ref, r0, rows):
    nh = o_ref.shape[1]
    for c0 in range(0, rows, SUBLANES):
        for h in range(nh):
            o_ref[pl.ds(r0 + c0, SUBLANES), h, :] = y[c0:c0 + SUBLANES, LANES * h:LANES * (h + 1)]


def _row_block(nc):
    for rb in (256, 128, 64, 32, 16, 8):
        if nc % rb == 0:
            return rb
    raise ValueError(f"chunk-row count {nc} must be a multiple of {SUBLANES}")


def _s5_states_body(x_ref, wb_ref, o_ref, vbuf):
    nc = x_ref.shape[0]
    rb = vbuf.shape[0]

    def blk(i, carry):
        r0 = pl.multiple_of(i * rb, rb)
        _gather_chunk_rows(x_ref, vbuf, r0, rb)
        o_ref[pl.ds(r0, rb), :] = jnp.dot(vbuf[...].astype(BF16), wb_ref[...], preferred_element_type=F32)
        return carry

    lax.fori_loop(0, nc // rb, blk, 0)


def s5_states_call(pxt, wb):
    nc, g, h, _ = pxt.shape
    ns = wb.shape[-1]
    rb = _row_block(nc)
    return pl.pallas_call(
        _s5_states_body,
        grid=(g,),
        in_specs=[pl.BlockSpec((nc, None, h, LANES), lambda i: (0, i, 0, 0)),
                  pl.BlockSpec((None, h * LANES, ns), lambda i: (i, 0, 0))],
        out_specs=pl.BlockSpec((nc, ns), lambda i: (0, i)),
        out_shape=jax.ShapeDtypeStruct((nc, g * ns), F32),
        scratch_shapes=[pltpu.VMEM((rb, h * LANES), F32)],
        compiler_params=_cparams(("arbitrary",)),
        name="s5_states",
    )(pxt, wb)


def _s5_recur_body(sl_ref, sc_ref, dr_ref, di_ref, hl_ref, hc_ref, *, half):
    ncl, ncc = sl_ref.shape[0], sc_ref.shape[0]
    shp = sl_ref.shape[1:]
    lane = lax.broadcasted_iota(jnp.int32, shp, 1)
    is_re = (lane % (2 * half)) < half
    is_fwd = (lane % (4 * half)) < 2 * half
    dr = jnp.broadcast_to(dr_ref[...], shp)
    di = jnp.broadcast_to(di_ref[...], shp)
    nl = shp[1]

    def step(hs, s):
        swap = jnp.where(is_re, pltpu.roll(hs, nl - half, 1), pltpu.roll(hs, half, 1))
        return dr * hs + di * swap + s

    def f_ctx(k, hs):
        hc_ref[k] = hs
        return step(hs, sc_ref[k])

    def f_lat(k, hs):
        hl_ref[k] = hs
        return step(hs, sl_ref[k])

    hs = lax.fori_loop(0, ncc, f_ctx, jnp.zeros(shp, F32))
    lax.fori_loop(0, ncl, f_lat, hs)

    def r_ctx(i, hs):
        k = ncc - 1 - i
        hc_ref[k] = jnp.where(is_fwd, hc_ref[k], hs)
        return step(hs, sc_ref[k])

    def r_lat(i, hs):
        k = ncl - 1 - i
        hl_ref[k] = jnp.where(is_fwd, hl_ref[k], hs)
        return step(hs, sl_ref[k])

    hs = lax.fori_loop(0, ncc, r_ctx, jnp.zeros(shp, F32))
    lax.fori_loop(0, ncl, r_lat, hs)


def s5_recur_call(sl, sc, dr, di, half):
    ncl, b, s = sl.shape
    ncc = sc.shape[0]
    lb = 1024 if s % 1024 == 0 else s
    return pl.pallas_call(
        functools.partial(_s5_recur_body, half=half),
        grid=(s // lb,),
        in_specs=[pl.BlockSpec((ncl, b, lb), lambda i: (0, 0, i)),
                  pl.BlockSpec((ncc, b, lb), lambda i: (0, 0, i)),
                  pl.BlockSpec((1, lb), lambda i: (0, i)),
                  pl.BlockSpec((1, lb), lambda i: (0, i))],
        out_specs=[pl.BlockSpec((ncl, b, lb), lambda i: (0, 0, i)),
                   pl.BlockSpec((ncc, b, lb), lambda i: (0, 0, i))],
        out_shape=[jax.ShapeDtypeStruct(sl.shape, F32), jax.ShapeDtypeStruct(sc.shape, F32)],
        compiler_params=_cparams(("arbitrary",)),
        name="s5_recur",
    )(sl, sc, dr, di)


def _s5_main_body(x_ref, hin_ref, wt_ref, wc_ref, o_ref, mt, vbuf):
    nc, nh = x_ref.shape[0], x_ref.shape[1]
    rb = vbuf.shape[0]

    def build(hi, carry):
        for ho in range(nh):
            w = wt_ref[pl.ds(hi * nh + ho, 1), :]
            t = pltpu.roll(jnp.broadcast_to(w, (LANES, 2 * LANES)), 0, 1, stride=1, stride_axis=0)
            mt[pl.ds(pl.multiple_of(hi * LANES, LANES), LANES), LANES * ho:LANES * (ho + 1)] = t[:, :LANES].astype(BF16)
        return carry

    lax.fori_loop(0, nh, build, 0)

    def blk(i, carry):
        r0 = pl.multiple_of(i * rb, rb)
        _gather_chunk_rows(x_ref, vbuf, r0, rb)
        y = jnp.dot(vbuf[...].astype(BF16), mt[...], preferred_element_type=F32)
        y = y + jnp.dot(hin_ref[pl.ds(r0, rb), :].astype(BF16), wc_ref[...], preferred_element_type=F32)
        vbuf[...] = y
        _scatter_chunk_rows(vbuf, o_ref, r0, rb)
        return carry

    lax.fori_loop(0, nc // rb, blk, 0)


def s5_main_call(pxt, hin, wtoe, wc):
    nc, g, h, _ = pxt.shape
    ns = wc.shape[1]
    rb = _row_block(nc)
    return pl.pallas_call(
        _s5_main_body,
        grid=(g,),
        in_specs=[pl.BlockSpec((nc, None, h, LANES), lambda i: (0, i, 0, 0)),
                  pl.BlockSpec((nc, ns), lambda i: (0, i)),
                  pl.BlockSpec((None, h * h, 2 * LANES), lambda i: (i, 0, 0)),
                  pl.BlockSpec((None, ns, h * LANES), lambda i: (i, 0, 0))],
        out_specs=pl.BlockSpec((nc, None, h, LANES), lambda i: (0, i, 0, 0)),
        out_shape=jax.ShapeDtypeStruct(pxt.shape, F32),
        scratch_shapes=[pltpu.VMEM((h * LANES, h * LANES), BF16), pltpu.VMEM((rb, h * LANES), F32)],
        compiler_params=_cparams(("arbitrary",)),
        name="s5_main",
    )(pxt, hin, wtoe, wc)


def _band(n, w):
    i = np.arange(n)
    return ((i[None, :] >= i[:, None] - w // 2) & (i[None, :] < i[:, None] + w // 2)).astype(np.float32)


def _cnt(n, w):
    i = np.arange(n)
    return (np.clip(i + w // 2, 0, n) - np.clip(i - w // 2, 0, n)).astype(np.float32)


def _pool_grid_body(wv_ref, x_ref, cb_ref, rc_ref, pw_ref, ps_ref, o_ref, cs, *, tb, halo):
    t = x_ref.shape[1]
    wh = wv_ref[pl.program_id(0)] // 2
    zeros = jnp.zeros((halo, LANES), F32)
    cs[0:halo, :] = zeros
    cs[halo + t:halo + t + halo, :] = zeros
    for i in range(t // tb):
        cs[halo + i * tb:halo + (i + 1) * tb, :] = jnp.dot(cb_ref[...], x_ref[0, i * tb:(i + 1) * tb, :],
                                                            preferred_element_type=F32)
    for i in range(t // tb):
        base = halo + i * tb

        def add(d, acc):
            return acc + cs[pl.ds(pl.multiple_of(base + (d - wh) * GRID_W, GRID_W), tb), :]

        acc = lax.fori_loop(0, 2 * wh, add, jnp.zeros((tb, LANES), F32))
        v = x_ref[0, i * tb:(i + 1) * tb, :].astype(F32)
        z = acc * rc_ref[i * tb:(i + 1) * tb, :] - v
        o = jnp.dot(z.astype(BF16), pw_ref[...], preferred_element_type=F32) * ps_ref[...]
        o_ref[0, i * tb:(i + 1) * tb, :] = o.astype(BF16)


def pool_grid_call(pxp, pw, ps):
    b, t, dp = pxp.shape
    ng = len(POOL_WINDOWS)
    cg = dp // ng
    rows = t // GRID_W
    tb = min(512, t)
    halo = (max(POOL_WINDOWS) // 2) * GRID_W
    cb = np.stack([np.kron(np.eye(tb // GRID_W, dtype=np.float32), _band(GRID_W, w)) for w in POOL_WINDOWS])
    rc = np.stack([1.0 / (np.repeat(_cnt(rows, w), GRID_W) * np.tile(_cnt(GRID_W, w), rows)) for w in POOL_WINDOWS])
    rc = np.broadcast_to(rc[:, :, None], (ng, t, cg)).astype(np.float32)
    gs = pltpu.PrefetchScalarGridSpec(
        num_scalar_prefetch=1,
        grid=(ng, b),
        in_specs=[pl.BlockSpec((1, t, cg), lambda g, i, wv: (i, 0, g)),
                  pl.BlockSpec((None, tb, tb), lambda g, i, wv: (g, 0, 0)),
                  pl.BlockSpec((None, t, cg), lambda g, i, wv: (g, 0, 0)),
                  pl.BlockSpec((None, cg, cg), lambda g, i, wv: (g, 0, 0)),
                  pl.BlockSpec((1, cg), lambda g, i, wv: (0, g))],
        out_specs=pl.BlockSpec((1, t, cg), lambda g, i, wv: (i, 0, g)),
        scratch_shapes=[pltpu.VMEM((t + 2 * halo, cg), F32)])
    return pl.pallas_call(
        functools.partial(_pool_grid_body, tb=tb, halo=halo),
        grid_spec=gs,
        out_shape=jax.ShapeDtypeStruct((b, t, dp), BF16),
        compiler_params=_cparams(("arbitrary", "arbitrary")),
        name="pool_grid",
    )(jnp.asarray(POOL_WINDOWS, jnp.int32), pxp, jnp.asarray(cb, BF16), jnp.asarray(rc), pw, ps)


def _pool_seq_body(x_ref, cb_ref, rc_ref, pw_ref, ps_ref, o_ref):
    v = x_ref[0]
    m = jnp.dot(cb_ref[...], v, preferred_element_type=F32) * rc_ref[...]
    z = m - v.astype(F32)
    o_ref[0] = (jnp.dot(z.astype(BF16), pw_ref[...], preferred_element_type=F32) * ps_ref[...]).astype(BF16)


def pool_seq_call(pcp, pw, ps):
    b, t, dp = pcp.shape
    ng = len(POOL_WINDOWS)
    cg = dp // ng
    cb = np.stack([_band(t, w) for w in POOL_WINDOWS])
    rc = np.stack([1.0 / _cnt(t, w) for w in POOL_WINDOWS])
    rc = np.broadcast_to(rc[:, :, None], (ng, t, cg)).astype(np.float32)
    return pl.pallas_call(
        _pool_seq_body,
        grid=(ng, b),
        in_specs=[pl.BlockSpec((1, t, cg), lambda g, i: (i, 0, g)),
                  pl.BlockSpec((None, t, t), lambda g, i: (g, 0, 0)),
                  pl.BlockSpec((None, t, cg), lambda g, i: (g, 0, 0)),
                  pl.BlockSpec((None, cg, cg), lambda g, i: (g, 0, 0)),
                  pl.BlockSpec((1, cg), lambda g, i: (0, g))],
        out_specs=pl.BlockSpec((1, t, cg), lambda g, i: (i, 0, g)),
        out_shape=jax.ShapeDtypeStruct((b, t, dp), BF16),
        compiler_params=_cparams(("arbitrary", "arbitrary")),
        name="pool_seq",
    )(pcp, jnp.asarray(cb, BF16), jnp.asarray(rc), pw, ps)


def _mixer_out_body(x_ref, yt_ref, pool_ref, mod_ref, wg_ref, wos_ref, wop_ref, gpost_ref, gpre_ref, wr_ref,
                    x1_ref, hx_ref, aff_ref):
    m = mod_ref[0]
    yt = jnp.concatenate([yt_ref[j] for j in range(yt_ref.shape[0])], axis=1)
    ge = jax.nn.gelu(yt)
    zt = jnp.dot(wg_ref[...], ge.astype(BF16), preferred_element_type=F32)
    s5o = (ge * jax.nn.sigmoid(zt)).astype(BF16)
    mix = lax.dot_general(s5o, wos_ref[...], (((0,), (0,)), ((), ())), preferred_element_type=F32)
    mix = mix + jnp.dot(pool_ref[0], wop_ref[...], preferred_element_type=F32)
    x1 = x_ref[0] + m[2:3] * _rms(mix, gpost_ref[...])
    x1_ref[0] = x1
    h2 = _rms(x1, gpre_ref[...]) * (1.0 + m[4:5]) + m[3:4]
    hx_ref[0] = h2.astype(BF16)
    lg = lax.dot_general(wr_ref[...], h2, (((1,), (1,)), ((), ())), preferred_element_type=F32, precision=HI)
    e = jnp.exp(lg - jnp.max(lg, axis=0, keepdims=True))
    aff_ref[0] = e / jnp.sum(e, axis=0, keepdims=True)


def mixer_out_call(x, yt, pool, mod3, mod_row, wglu_t, wo_s, wo_p, g_post, g_pre, wr_t):
    b, t, d = x.shape
    ds, dp = wo_s.shape[0], wo_p.shape[0]
    ne = wr_t.shape[0]
    tm = min(512, t)
    nj = tm // LANES
    mod_idx = (lambda i, j: (i, 0, 0)) if mod_row is None else (lambda i, j: (mod_row, 0, 0))
    const = lambda i, j: (0, 0)
    return pl.pallas_call(
        _mixer_out_body,
        grid=(b, t // tm),
        in_specs=[pl.BlockSpec((1, tm, d), lambda i, j: (i, j, 0)),
                  pl.BlockSpec((nj, None, ds, LANES), lambda i, j: (j, i, 0, 0)),
                  pl.BlockSpec((1, tm, dp), lambda i, j: (i, j, 0)),
                  pl.BlockSpec((1, 6, d), mod_idx),
                  pl.BlockSpec((ds, ds), const),
                  pl.BlockSpec((ds, d), const),
                  pl.BlockSpec((dp, d), const),
                  pl.BlockSpec((1, d), const),
                  pl.BlockSpec((1, d), const),
                  pl.BlockSpec((ne, d), const)],
        out_specs=[pl.BlockSpec((1, tm, d), lambda i, j: (i, j, 0)),
                   pl.BlockSpec((1, tm, d), lambda i, j: (i, j, 0)),
                   pl.BlockSpec((1, ne, tm), lambda i, j: (i, 0, j))],
        out_shape=[jax.ShapeDtypeStruct((b, t, d), F32),
                   jax.ShapeDtypeStruct((b, t, d), BF16),
                   jax.ShapeDtypeStruct((b, ne, t), F32)],
        compiler_params=_cparams(("arbitrary", "arbitrary")),
        name="mixer_out",
    )(x, yt, pool, mod3, wglu_t, wo_s, wo_p, g_post.reshape(1, d), g_pre.reshape(1, d), wr_t)


def _select_body(aff_ref, tri_ref, pos_ref, offs_ref, *, cap, tile):
    a = aff_ref[0]
    ne, t = a.shape
    bits = pltpu.bitcast(a, jnp.int32)

    def count_ge(thr):
        return jnp.sum((bits >= thr).astype(jnp.int32), axis=1, keepdims=True)

    def bis(_, c):
        lo, hi = c
        mid = lo + lax.shift_right_logical(hi - lo, 1)
        ok = count_ge(mid) >= cap
        return jnp.where(ok, mid, lo), jnp.where(ok, hi, mid)

    lo0 = jnp.zeros((ne, 1), jnp.int32)
    hi0 = jnp.full((ne, 1), 0x7F800001, jnp.int32)
    thr, _ = lax.fori_loop(0, 32, bis, (lo0, hi0))
    gt = bits > thr
    eq = bits == thr
    need = (cap - jnp.sum(gt.astype(jnp.int32), axis=1, keepdims=True)).astype(F32)
    tri = tri_ref[...]
    nblk = t // tile

    def cumsum(mask_f32):
        run = jnp.zeros((ne, 1), F32)
        pieces, starts = [], []
        for j in range(nblk):
            starts.append(run)
            cj = jnp.dot(mask_f32[:, tile * j:tile * (j + 1)].astype(BF16), tri, preferred_element_type=F32) + run
            pieces.append(cj)
            run = cj[:, tile - 1:tile]
        starts.append(run)
        return jnp.concatenate(pieces, axis=1) if nblk > 1 else pieces[0], starts

    ceq, _ = cumsum(jnp.where(eq, 1.0, 0.0))
    sel = gt | (eq & (ceq <= need))
    csel, starts = cumsum(jnp.where(sel, 1.0, 0.0))
    pos_ref[0] = jnp.where(sel, csel.astype(jnp.int32) - 1, -1)
    lane = lax.broadcasted_iota(jnp.int32, (ne, LANES), 1)
    offs = jnp.zeros((ne, LANES), F32)
    for j, s in enumerate(starts):
        offs = jnp.where(lane == j, s, offs)
    offs_ref[0] = offs.astype(jnp.int32)


def select_call(afft, cap, tile):
    b, ne, t = afft.shape
    assert t % tile == 0 and t // tile + 1 <= LANES
    tri = jnp.asarray(np.triu(np.ones((tile, tile), np.float32)), BF16)
    pos, offs = pl.pallas_call(
        functools.partial(_select_body, cap=cap, tile=tile),
        grid=(b,),
        in_specs=[pl.BlockSpec((1, ne, t), lambda i: (i, 0, 0)),
                  pl.BlockSpec((tile, tile), lambda i: (0, 0))],
        out_specs=[pl.BlockSpec((1, ne, t), lambda i: (i, 0, 0)),
                   pl.BlockSpec((1, ne, LANES), lambda i: (i, 0, 0))],
        out_shape=[jax.ShapeDtypeStruct((b, ne, t), jnp.int32),
                   jax.ShapeDtypeStruct((b, ne, LANES), jnp.int32)],
        compiler_params=_cparams(("arbitrary",)),
        name="select",
    )(afft, tri)
    nt = t // tile
    offs_flat = jnp.transpose(offs[:, :, :nt + 1], (0, 2, 1)).reshape(-1)
    return pos, offs_flat


def _align_down(v):
    return lax.shift_left(lax.shift_right_logical(v, 4), 4)


def _tile_slots(offs_ref, b, j, e, nt, ne):
    s0 = offs_ref[(b * (nt + 1) + j) * ne + e]
    s1 = offs_ref[(b * (nt + 1) + j + 1) * ne + e]
    return s0, s1


def _one_hot_t(pos_row, base, win, value=None):
    tt = pos_row.shape[1]
    k = lax.broadcasted_iota(jnp.int32, (win, tt), 0)
    hit = k == (pos_row - base)
    if value is None:
        return jnp.where(hit, 1.0, 0.0).astype(BF16)
    return jnp.where(hit, value, 0.0).astype(BF16)


def _dispatch_body(offs_ref, hx_ref, pos_ref, x_hbm, stage, stage2, carry, sem, sem2, *, nt, ne, win, cap):
    i = pl.program_id(0)
    n_steps = pl.num_programs(0)
    b, j = i // nt, i % nt
    slot = i % 2
    d = hx_ref.shape[2]

    @pl.when(j == 0)
    def _():
        carry[...] = jnp.zeros(carry.shape, carry.dtype)

    hx = hx_ref[0]
    s0s, s1s, a0s = [], [], []
    for e in range(ne):
        s0, s1 = _tile_slots(offs_ref, b, j, e, nt, ne)
        s0s.append(s0)
        s1s.append(s1)
        a0s.append(_align_down(s0))

    def windows(c):
        q = jnp.concatenate([_one_hot_t(pos_ref[0, e:e + 1, :], a0s[e] + c * win, win) for e in range(ne)], axis=0)
        return jnp.dot(q, hx, preferred_element_type=F32)

    def chunk_copy(buf, e, c, semref):
        dst = x_hbm.at[e, b, pl.ds(pl.multiple_of(a0s[e] + c * win, BF16_ROWS), win), :]
        return pltpu.make_async_copy(buf.at[e], dst, semref.at[e])

    xw = windows(0).reshape(ne, win, d)
    head = xw[:, :BF16_ROWS, :] + carry[...].astype(F32)
    stage[slot, :, :BF16_ROWS, :] = head.astype(BF16)
    stage[slot, :, BF16_ROWS:, :] = xw[:, BF16_ROWS:, :].astype(BF16)

    @pl.when(i > 0)
    def _():
        for e in range(ne):
            pltpu.make_async_copy(stage.at[1 - slot, e], x_hbm.at[e, 0, pl.ds(0, win), :], sem.at[1 - slot, e]).wait()

    for e in range(ne):
        chunk_copy(stage.at[slot], e, 0, sem.at[slot]).start()
        off = _align_down(s1s[e]) - a0s[e]

        @pl.when(off < win)
        def _():
            carry[e] = stage[slot, e, pl.ds(pl.multiple_of(off, BF16_ROWS), BF16_ROWS), :]

    nch = [lax.div(s1s[e] - a0s[e], win) + 1 for e in range(ne)]
    nmax = functools.reduce(jnp.maximum, nch)

    def extra(c, cr):
        xc = windows(c).reshape(ne, win, d)
        stage2[...] = xc.astype(BF16)
        for e in range(ne):
            @pl.when(c < nch[e])
            def _():
                cp = chunk_copy(stage2, e, c, sem2)
                cp.start()
                cp.wait()
                off = _align_down(s1s[e]) - a0s[e] - c * win

                @pl.when(jnp.logical_and(off >= 0, off < win))
                def _():
                    carry[e] = stage2[e, pl.ds(pl.multiple_of(off, BF16_ROWS), BF16_ROWS), :]
        return cr

    lax.fori_loop(1, nmax, extra, 0)

    if nt > 1:
        assert cap >= win

        @pl.when(j == 0)
        def _():
            stage2[0] = jnp.zeros((win, d), BF16)
            tails = [pltpu.make_async_copy(stage2.at[0], x_hbm.at[e, b, pl.ds(cap, win), :], sem2.at[e])
                     for e in range(ne)]
            for cp in tails:
                cp.start()
            for cp in tails:
                cp.wait()

    @pl.when(i == n_steps - 1)
    def _():
        for e in range(ne):
            pltpu.make_async_copy(stage.at[slot, e], x_hbm.at[e, 0, pl.ds(0, win), :], sem.at[slot, e]).wait()


def dispatch_call(hx, pos, offs_flat, cap, tile, win):
    b, t, d = hx.shape
    ne = pos.shape[1]
    nt = t // tile
    cpad = cap + win if nt > 1 else win
    assert nt > 1 or cap < win
    gs = pltpu.PrefetchScalarGridSpec(
        num_scalar_prefetch=1,
        grid=(b * nt,),
        in_specs=[pl.BlockSpec((1, tile, d), lambda i, o: (i // nt, i % nt, 0)),
                  pl.BlockSpec((1, ne, tile), lambda i, o: (i // nt, 0, i % nt))],
        out_specs=pl.BlockSpec(memory_space=pl.ANY),
        scratch_shapes=[pltpu.VMEM((2, ne, win, d), BF16), pltpu.VMEM((ne, win, d), BF16),
                        pltpu.VMEM((ne, BF16_ROWS, d), BF16),
                        pltpu.SemaphoreType.DMA((2, ne)), pltpu.SemaphoreType.DMA((ne,))])
    return pl.pallas_call(
        functools.partial(_dispatch_body, nt=nt, ne=ne, win=win, cap=cap),
        grid_spec=gs,
        out_shape=jax.ShapeDtypeStruct((ne, b, cpad, d), BF16),
        compiler_params=_cparams(("arbitrary",)),
        name="dispatch",
    )(offs_flat, hx, pos)


def _ffn_body(x_ref, wg_ref, wu_ref, wd_ref, y_ref, *, rows, rb):
    for r in range(0, rows, rb):
        xb = x_ref[r:r + rb, :]
        a = jnp.dot(xb, wg_ref[...], preferred_element_type=F32)
        g = jnp.dot(xb, wu_ref[...], preferred_element_type=F32)
        h = (jax.nn.silu(a) * g).astype(BF16)
        y_ref[r:r + rb, :] = jnp.dot(h, wd_ref[...], preferred_element_type=F32).astype(BF16)
    if y_ref.shape[0] > rows:
        y_ref[rows:, :] = jnp.zeros((y_ref.shape[0] - rows, y_ref.shape[1]), BF16)


def ffn_call(xs, wg, wu, wd, cap):
    ne, b, cpad, d = xs.shape
    f = wg.shape[2]
    if cap % 256 == 0:
        grid = (ne, b)
        x_spec = pl.BlockSpec((None, None, cap, d), lambda e, i: (e, i, 0, 0))
        y_spec = pl.BlockSpec((None, None, cpad, d), lambda e, i: (e, i, 0, 0))
        wmap = lambda e, i: (e, 0, 0)
        body = functools.partial(_ffn_body, rows=cap, rb=256)
        sem = ("arbitrary", "arbitrary")
        args, shape = (xs, wg, wu, wd), xs.shape
    else:
        rows = b * cpad
        grid = (ne,)
        x_spec = pl.BlockSpec((None, rows, d), lambda e: (e, 0, 0))
        y_spec = pl.BlockSpec((None, rows, d), lambda e: (e, 0, 0))
        wmap = lambda e: (e, 0, 0)
        body = functools.partial(_ffn_body, rows=rows, rb=rows)
        sem = ("arbitrary",)
        args, shape = (xs.reshape(ne, rows, d), wg, wu, wd), (ne, rows, d)
    y = pl.pallas_call(
        body,
        grid=grid,
        in_specs=[x_spec, pl.BlockSpec((None, d, f), wmap), pl.BlockSpec((None, d, f), wmap),
                  pl.BlockSpec((None, f, d), wmap)],
        out_specs=y_spec,
        out_shape=jax.ShapeDtypeStruct(shape, BF16),
        compiler_params=_cparams(sem),
        name="ffn",
    )(*args)
    return y.reshape(xs.shape)


def _combine_body(offs_ref, x1_ref, pos_ref, aff_ref, mod_ref, g_ref, y_hbm, o_ref, ywin, ywin2, sem, sem2,
                  *, nt, ne, win):
    i = pl.program_id(0)
    n_steps = pl.num_programs(0)
    b, j = i // nt, i % nt
    slot = i % 2
    d = x1_ref.shape[2]
    cpad = y_hbm.shape[2]

    def window_copy(step, e, sl):
        bb, jj = step // nt, step % nt
        a = _align_down(offs_ref[(bb * (nt + 1) + jj) * ne + e])
        return pltpu.make_async_copy(y_hbm.at[e, bb, pl.ds(pl.multiple_of(a, BF16_ROWS), win), :],
                                     ywin.at[sl, e], sem.at[sl, e])

    @pl.when(i == 0)
    def _():
        for e in range(ne):
            window_copy(i, e, slot).start()

    @pl.when(i + 1 < n_steps)
    def _():
        for e in range(ne):
            window_copy(i + 1, e, 1 - slot).start()

    s1s, a0s = [], []
    for e in range(ne):
        s0, s1 = _tile_slots(offs_ref, b, j, e, nt, ne)
        s1s.append(s1)
        a0s.append(_align_down(s0))

    def weights(c):
        return jnp.concatenate(
            [_one_hot_t(pos_ref[0, e:e + 1, :], a0s[e] + c * win, win, aff_ref[0, e:e + 1, :]) for e in range(ne)],
            axis=0)

    pt = weights(0)
    for e in range(ne):
        window_copy(i, e, slot).wait()
    moe = lax.dot_general(pt, ywin[slot].reshape(ne * win, d), (((0,), (0,)), ((), ())), preferred_element_type=F32)

    nch = [lax.div(s1s[e] - a0s[e], win) + 1 for e in range(ne)]
    nmax = functools.reduce(jnp.maximum, nch)

    def extra(c, acc):
        for e in range(ne):
            a = jnp.minimum(a0s[e] + c * win, cpad - win)
            cp = pltpu.make_async_copy(y_hbm.at[e, b, pl.ds(pl.multiple_of(a, BF16_ROWS), win), :],
                                       ywin2.at[e], sem2.at[e])
            cp.start()
            cp.wait()
        return acc + lax.dot_general(weights(c), ywin2[...].reshape(ne * win, d), (((0,), (0,)), ((), ())),
                                     preferred_element_type=F32)

    moe = lax.fori_loop(1, nmax, extra, moe)
    m = mod_ref[0]
    o_ref[0] = x1_ref[0] + m[5:6] * _rms(moe, g_ref[...])


def combine_call(x1, ys, pos, afft, offs_flat, mod3, mod_row, g_post, tile, win):
    b, t, d = x1.shape
    ne = pos.shape[1]
    nt = t // tile
    mod_idx = (lambda i, o: (i // nt, 0, 0)) if mod_row is None else (lambda i, o: (mod_row, 0, 0))
    gs = pltpu.PrefetchScalarGridSpec(
        num_scalar_prefetch=1,
        grid=(b * nt,),
        in_specs=[pl.BlockSpec((1, tile, d), lambda i, o: (i // nt, i % nt, 0)),
                  pl.BlockSpec((1, ne, tile), lambda i, o: (i // nt, 0, i % nt)),
                  pl.BlockSpec((1, ne, tile), lambda i, o: (i // nt, 0, i % nt)),
                  pl.BlockSpec((1, 6, d), mod_idx),
                  pl.BlockSpec((1, d), lambda i, o: (0, 0)),
                  pl.BlockSpec(memory_space=pl.ANY)],
        out_specs=pl.BlockSpec((1, tile, d), lambda i, o: (i // nt, i % nt, 0)),
        scratch_shapes=[pltpu.VMEM((2, ne, win, d), BF16), pltpu.VMEM((ne, win, d), BF16),
                        pltpu.SemaphoreType.DMA((2, ne)), pltpu.SemaphoreType.DMA((ne,))])
    return pl.pallas_call(
        functools.partial(_combine_body, nt=nt, ne=ne, win=win),
        grid_spec=gs,
        out_shape=jax.ShapeDtypeStruct((b, t, d), F32),
        compiler_params=_cparams(("arbitrary",)),
        name="combine",
    )(offs_flat, x1, pos, afft, mod3, g_post.reshape(1, d), ys)


def ec_moe_call(x1, hx, afft, mod3, mod_row, g_post, wg, wu, wd):
    b, t, d = x1.shape
    ne = afft.shape[1]
    cap = EC_CAPACITY_FACTOR * t // ne
    tile = min(256, t)
    win = min(cap, 2 * EC_CAPACITY_FACTOR * tile // ne) + BF16_ROWS
    win = -(-win // BF16_ROWS) * BF16_ROWS
    pos, offs_flat = select_call(afft, cap, tile)
    xs = dispatch_call(hx, pos, offs_flat, cap, tile, win)
    ys = ffn_call(xs, wg, wu, wd, cap)
    return combine_call(x1, ys, pos, afft, offs_flat, mod3, mod_row, g_post, tile, win)


def kernel(x, c, ctx, c_ctx, w_mod, b_mod, g_mix_pre, g_mix_post, g_ffn_pre, g_ffn_post, w_in, s5_lam_re, s5_lam_im,
           s5_log_dt, s5_b_re, s5_b_im, s5_c_re, s5_c_im, s5_d, s5_w_glu, pool_w, pool_scale, w_out, w_router,
           w_gate, w_up, w_down):
    depth = w_mod.shape[0]
    b, t, d = x.shape
    _, _, g, p, h = s5_b_re.shape
    ds = g * h
    dp = d - ds
    assert t % min(512, t) == 0 and t % LANES == 0 and t % GRID_W == 0
    assert ctx.shape[1] % LANES == 0 and 2 * p == LANES
    nmod = w_mod.shape[2] // d
    rows = -(-(b + 1) // SUBLANES) * SUBLANES
    cc = jnp.zeros((rows, d), F32).at[:b].set(c.astype(F32)).at[b].set(c_ctx.astype(F32))
    ncl, ncc = t // LANES, ctx.shape[1] // LANES

    for l in range(depth):
        last = l == depth - 1
        mod3 = mod_call(cc, w_mod[l].astype(F32), b_mod[l].astype(F32)).reshape(rows, nmod, d)
        ws_t = jnp.transpose(w_in[l][:, :ds]).astype(BF16)
        wp = w_in[l][:, ds:].astype(BF16)
        wtoe, wb, wc, dr, di = s5_tables(s5_lam_re[l], s5_lam_im[l], s5_log_dt[l], s5_b_re[l], s5_b_im[l],
                                         s5_c_re[l], s5_c_im[l], s5_d[l])
        pw = pool_w[l].astype(BF16)
        ps = pool_scale[l].astype(F32).reshape(1, dp)
        wglu_t = jnp.transpose(s5_w_glu[l]).astype(BF16)
        wo_s, wo_p = w_out[l][:ds].astype(BF16), w_out[l][ds:].astype(BF16)
        wr_t = jnp.transpose(w_router[l]).astype(F32)
        wg, wu, wd = w_gate[l].astype(BF16), w_up[l].astype(BF16), w_down[l].astype(BF16)

        pxt, pxp = mixer_in_call(x, mod3, None, g_mix_pre[l], ws_t, wp)
        pct, pcp = mixer_in_call(ctx, mod3, b, g_mix_pre[l], ws_t, wp)
        sl = s5_states_call(pxt.reshape(ncl * b, g, h, LANES), wb)
        sc = s5_states_call(pct.reshape(ncc * b, g, h, LANES), wb)
        hl, hc = s5_recur_call(sl.reshape(ncl, b, -1), sc.reshape(ncc, b, -1), dr, di, p)
        yt = s5_main_call(pxt.reshape(ncl * b, g, h, LANES), hl.reshape(ncl * b, -1), wtoe, wc)
        pool_x = pool_grid_call(pxp, pw, ps)
        x1, hx, afft = mixer_out_call(x, yt.reshape(ncl, b, ds, LANES), pool_x, mod3, None, wglu_t, wo_s, wo_p,
                                      g_mix_post[l], g_ffn_pre[l], wr_t)
        x = ec_moe_call(x1, hx, afft, mod3, None, g_ffn_post[l], wg, wu, wd)
        if not last:
            yc = s5_main_call(pct.reshape(ncc * b, g, h, LANES), hc.reshape(ncc * b, -1), wtoe, wc)
            pool_c = pool_seq_call(pcp, pw, ps)
            c1, hcx, affc = mixer_out_call(ctx, yc.reshape(ncc, b, ds, LANES), pool_c, mod3, b, wglu_t, wo_s, wo_p,
                                           g_mix_post[l], g_ffn_pre[l], wr_t)
            ctx = ec_moe_call(c1, hcx, affc, mod3, b, g_ffn_post[l], wg, wu, wd)
    return x
```

```python
import functools
import math

import numpy as np
import jax
import jax.numpy as jnp
from jax import lax
from jax.experimental import pallas as pl
from jax.experimental.pallas import tpu as pltpu

GRID_W = 64
POOL_WINDOWS = (2, 4, 8, 16)
EC_CAPACITY_FACTOR = 2
RMS_EPS = 1e-6

LANES = 128
SUBLANES = 8
BF16_ROWS = 16
VMEM_LIMIT = 56 * 1024 * 1024

F32 = jnp.float32
BF16 = jnp.bfloat16
HI = lax.Precision.HIGHEST


def _cparams(sem):
    return pltpu.CompilerParams(dimension_semantics=sem, vmem_limit_bytes=VMEM_LIMIT)


def _rms(v, g):
    return v * lax.rsqrt(jnp.mean(v * v, axis=-1, keepdims=True) + RMS_EPS) * g


def _mod_body(c_ref, w_ref, b_ref, o_ref):
    s = jax.nn.silu(c_ref[...])
    o_ref[...] = jnp.dot(s, w_ref[...], preferred_element_type=F32, precision=HI) + b_ref[...]


def mod_call(cc, wm, bm):
    r, d = cc.shape
    n = wm.shape[1]
    return pl.pallas_call(
        _mod_body,
        grid=(n // d,),
        in_specs=[pl.BlockSpec((r, d), lambda j: (0, 0)),
                  pl.BlockSpec((d, d), lambda j: (0, j)),
                  pl.BlockSpec((1, d), lambda j: (0, j))],
        out_specs=pl.BlockSpec((r, d), lambda j: (0, j)),
        out_shape=jax.ShapeDtypeStruct((r, n), F32),
        compiler_params=_cparams(("arbitrary",)),
        name="mod",
    )(cc, wm, bm.reshape(1, n))


def _mixer_in_body(x_ref, mod_ref, g_ref, ws_ref, wp_ref, pxt_ref, pxp_ref):
    m = mod_ref[0]
    h = _rms(x_ref[0], g_ref[...]) * (1.0 + m[1:2]) + m[0:1]
    hb = h.astype(BF16)
    pxp_ref[0] = jnp.dot(hb, wp_ref[...], preferred_element_type=F32).astype(BF16)
    pt = lax.dot_general(ws_ref[...], hb, (((1,), (1,)), ((), ())), preferred_element_type=F32)
    for j in range(pxt_ref.shape[0]):
        pxt_ref[j] = pt[:, LANES * j:LANES * (j + 1)]


def mixer_in_call(x, mod3, mod_row, g, ws_t, wp):
    b, t, d = x.shape
    ds, dp = ws_t.shape[0], wp.shape[1]
    tm = min(512, t)
    nj = tm // LANES
    mod_idx = (lambda i, j: (i, 0, 0)) if mod_row is None else (lambda i, j: (mod_row, 0, 0))
    return pl.pallas_call(
        _mixer_in_body,
        grid=(b, t // tm),
        in_specs=[pl.BlockSpec((1, tm, d), lambda i, j: (i, j, 0)),
                  pl.BlockSpec((1, 6, d), mod_idx),
                  pl.BlockSpec((1, d), lambda i, j: (0, 0)),
                  pl.BlockSpec((ds, d), lambda i, j: (0, 0)),
                  pl.BlockSpec((d, dp), lambda i, j: (0, 0))],
        out_specs=[pl.BlockSpec((nj, None, ds, LANES), lambda i, j: (j, i, 0, 0)),
                   pl.BlockSpec((1, tm, dp), lambda i, j: (i, j, 0))],
        out_shape=[jax.ShapeDtypeStruct((t // LANES, b, ds, LANES), F32),
                   jax.ShapeDtypeStruct((b, t, dp), BF16)],
        compiler_params=_cparams(("arbitrary", "arbitrary")),
        name="mixer_in",
    )(x, mod3, g.reshape(1, d), ws_t, wp)


def s5_tables(lam_re, lam_im, log_dt, b_re, b_im, c_re, c_im, d_skip):
    n_dir, g, p = lam_re.shape
    h = b_re.shape[-1]
    lc = LANES
    lam_re, lam_im = lam_re.astype(F32), lam_im.astype(F32)
    dt = jnp.exp(log_dt.astype(F32))[..., None]
    zr, zi = lam_re * dt, lam_im * dt
    er = jnp.exp(zr)
    ar, ai = er * jnp.cos(zi), er * jnp.sin(zi)
    den = lam_re * lam_re + lam_im * lam_im
    qr = ((ar - 1.0) * lam_re + ai * lam_im) / den
    qi = (ai * lam_re - (ar - 1.0) * lam_im) / den
    br, bi = b_re.astype(F32), b_im.astype(F32)
    bbr = qr[..., None] * br - qi[..., None] * bi
    bbi = qr[..., None] * bi + qi[..., None] * br
    cr, ci = c_re.astype(F32), c_im.astype(F32)
    jj = jnp.arange(lc + 1, dtype=F32)
    pe = jnp.exp(zr[..., None] * jj)
    pr, pi = pe * jnp.cos(zi[..., None] * jj), pe * jnp.sin(zi[..., None] * jj)
    e_r = jnp.einsum('dgop,dgpi->dgpoi', cr, bbr, precision=HI) - jnp.einsum('dgop,dgpi->dgpoi', ci, bbi, precision=HI)
    e_i = jnp.einsum('dgop,dgpi->dgpoi', cr, bbi, precision=HI) + jnp.einsum('dgop,dgpi->dgpoi', ci, bbr, precision=HI)
    kk = (jnp.einsum('dgpj,dgpoi->dgjoi', pr[..., :lc], e_r, precision=HI)
          - jnp.einsum('dgpj,dgpoi->dgjoi', pi[..., :lc], e_i, precision=HI))
    kf = jnp.transpose(kk[0], (0, 3, 2, 1))
    kr = jnp.transpose(kk[1], (0, 3, 2, 1))
    dsk = d_skip.astype(F32).reshape(g, h)
    eye = jnp.eye(h, dtype=F32)[None] * dsk[:, :, None]
    w0 = kf[..., 0] + kr[..., 0] + eye
    wtoe = jnp.concatenate([w0[..., None], kf[..., 1:], jnp.zeros_like(w0)[..., None], kr[..., :0:-1]], axis=-1)
    wtoe = wtoe.reshape(g, h * h, 2 * lc)
    tp = lambda a: jnp.swapaxes(a, -1, -2)
    lanes = lambda parts: jnp.concatenate(parts, axis=-1)
    prf, pif, prr, pir = tp(pr[0]), tp(pi[0]), tp(pr[1]), tp(pi[1])
    q_f_r, q_f_i = prf[:, lc - 1::-1][:, :lc], pif[:, lc - 1::-1][:, :lc]
    q_r_r, q_r_i = prr[:, :lc], pir[:, :lc]
    i_f_r, i_f_i = prf[:, 1:lc + 1], pif[:, 1:lc + 1]
    i_r_r, i_r_i = prr[:, lc:0:-1], pir[:, lc:0:-1]
    pw = jnp.stack([lanes([q_f_r, q_f_r, q_r_r, q_r_r]), lanes([q_f_i, q_f_i, q_r_i, q_r_i]),
                    lanes([i_f_r, i_f_i, i_r_r, i_r_i]), lanes([i_f_i, i_f_r, i_r_i, i_r_r])], axis=1)
    b_r, b_i = tp(bbr), tp(bbi)
    hb = jnp.stack([lanes([b_r[0], b_i[0], b_r[1], b_i[1]]), lanes([-b_i[0], b_r[0], -b_i[1], b_r[1]]),
                    lanes([cr[0], -cr[0], cr[1], -cr[1]]), lanes([-ci[0], -ci[0], -ci[1], -ci[1]])], axis=1)
    dre = jnp.stack([pr[0][..., lc], pr[0][..., lc], pr[1][..., lc], pr[1][..., lc]], axis=1)
    dim_ = jnp.stack([-pi[0][..., lc], pi[0][..., lc], -pi[1][..., lc], pi[1][..., lc]], axis=1)
    return wtoe, pw, hb, dre.reshape(1, g * 4 * p), dim_.reshape(1, g * 4 * p)


def _gather_chunk_rows(x_ref, vbuf, r0, rows):
    nh = x_ref.shape[1]
    for c0 in range(0, rows, SUBLANES):
        for h in range(nh):
            vbuf[c0:c0 + SUBLANES, LANES * h:LANES * (h + 1)] = x_ref[pl.ds(r0 + c0, SUBLANES), h, :]


def _scatter_chunk_rows(y, o_ref, r0, rows):
    nh = o_ref.shape[1]
    for c0 in range(0, rows, SUBLANES):
        for h in range(nh):
            o_ref[pl.ds(r0 + c0, SUBLANES), h, :] = y[c0:c0 + SUBLANES, LANES * h:LANES * (h + 1)]


def _row_block(nc):
    for rb in (256, 128, 64, 32, 16, 8):
        if nc % rb == 0:
            return rb
    raise ValueError(f"chunk-row count {nc} must be a multiple of {SUBLANES}")


def _build_state_table(pw_ref, hb_ref, k, tab):
    nh = hb_ref.shape[1]
    pa, pb = pw_ref[k], pw_ref[k + 1]
    for h in range(nh):
        tab[LANES * h:LANES * (h + 1), :] = (hb_ref[k, h:h + 1, :] * pa + hb_ref[k + 1, h:h + 1, :] * pb).astype(BF16)


def _for_row_blocks(nc, rb_max, fn):
    rb = min(rb_max, nc)
    assert nc % rb == 0

    def blk(i, carry):
        fn(pl.multiple_of(i * rb, rb), rb)
        return carry

    lax.fori_loop(0, nc // rb, blk, 0)


def _s5_states_body(xl_ref, xc_ref, pw_ref, hb_ref, ol_ref, oc_ref, wb, vbuf):
    _build_state_table(pw_ref, hb_ref, 0, wb)
    for x_ref, o_ref in ((xl_ref, ol_ref), (xc_ref, oc_ref)):
        def blk(r0, rb, x_ref=x_ref, o_ref=o_ref):
            _gather_chunk_rows(x_ref, vbuf, r0, rb)
            o_ref[pl.ds(r0, rb), :] = jnp.dot(vbuf[0:rb, :].astype(BF16), wb[...], preferred_element_type=F32)

        _for_row_blocks(x_ref.shape[0], vbuf.shape[0], blk)


def s5_states_call(pxt, pct, pw, hb):
    ncl, g, h, _ = pxt.shape
    ncc = pct.shape[0]
    ns = pw.shape[-1]
    rb = _row_block(ncl)
    tab = lambda i: (i, 0, 0, 0)
    return pl.pallas_call(
        _s5_states_body,
        grid=(g,),
        in_specs=[pl.BlockSpec((ncl, None, h, LANES), lambda i: (0, i, 0, 0)),
                  pl.BlockSpec((ncc, None, h, LANES), lambda i: (0, i, 0, 0)),
                  pl.BlockSpec((None, 4, LANES, ns), tab),
                  pl.BlockSpec((None, 4, h, ns), tab)],
        out_specs=[pl.BlockSpec((ncl, ns), lambda i: (0, i)), pl.BlockSpec((ncc, ns), lambda i: (0, i))],
        out_shape=[jax.ShapeDtypeStruct((ncl, g * ns), F32), jax.ShapeDtypeStruct((ncc, g * ns), F32)],
        scratch_shapes=[pltpu.VMEM((h * LANES, ns), BF16), pltpu.VMEM((rb, h * LANES), F32)],
        compiler_params=_cparams(("arbitrary",)),
        name="s5_states",
    )(pxt, pct, pw, hb)


def _s5_recur_body(sl_ref, sc_ref, dr_ref, di_ref, hl_ref, hc_ref, *, half):
    ncl, ncc = sl_ref.shape[0], sc_ref.shape[0]
    shp = sl_ref.shape[1:]
    lane = lax.broadcasted_iota(jnp.int32, shp, 1)
    is_re = (lane % (2 * half)) < half
    is_fwd = (lane % (4 * half)) < 2 * half
    dr = jnp.broadcast_to(dr_ref[...], shp)
    di = jnp.broadcast_to(di_ref[...], shp)
    nl = shp[1]

    def step(hs, s):
        swap = jnp.where(is_re, pltpu.roll(hs, nl - half, 1), pltpu.roll(hs, half, 1))
        return dr * hs + di * swap + s

    def f_ctx(k, hs):
        hc_ref[k] = hs
        return step(hs, sc_ref[k])

    def f_lat(k, hs):
        hl_ref[k] = hs
        return step(hs, sl_ref[k])

    hs = lax.fori_loop(0, ncc, f_ctx, jnp.zeros(shp, F32))
    lax.fori_loop(0, ncl, f_lat, hs)

    def r_ctx(i, hs):
        k = ncc - 1 - i
        hc_ref[k] = jnp.where(is_fwd, hc_ref[k], hs)
        return step(hs, sc_ref[k])

    def r_lat(i, hs):
        k = ncl - 1 - i
        hl_ref[k] = jnp.where(is_fwd, hl_ref[k], hs)
        return step(hs, sl_ref[k])

    hs = lax.fori_loop(0, ncc, r_ctx, jnp.zeros(shp, F32))
    lax.fori_loop(0, ncl, r_lat, hs)


def s5_recur_call(sl, sc, dr, di, half):
    ncl, b, s = sl.shape
    ncc = sc.shape[0]
    lb = 1024 if s % 1024 == 0 else s
    return pl.pallas_call(
        functools.partial(_s5_recur_body, half=half),
        grid=(s // lb,),
        in_specs=[pl.BlockSpec((ncl, b, lb), lambda i: (0, 0, i)),
                  pl.BlockSpec((ncc, b, lb), lambda i: (0, 0, i)),
                  pl.BlockSpec((1, lb), lambda i: (0, i)),
                  pl.BlockSpec((1, lb), lambda i: (0, i))],
        out_specs=[pl.BlockSpec((ncl, b, lb), lambda i: (0, 0, i)),
                   pl.BlockSpec((ncc, b, lb), lambda i: (0, 0, i))],
        out_shape=[jax.ShapeDtypeStruct(sl.shape, F32), jax.ShapeDtypeStruct(sc.shape, F32)],
        compiler_params=_cparams(("arbitrary",)),
        name="s5_recur",
    )(sl, sc, dr, di)


def _s5_main_body(*refs, n_sets):
    x_refs, hin_refs = refs[:n_sets], refs[n_sets:2 * n_sets]
    wt_ref, pw_ref, hb_ref = refs[2 * n_sets:2 * n_sets + 3]
    o_refs = refs[2 * n_sets + 3:3 * n_sets + 3]
    mt, wct, vbuf = refs[3 * n_sets + 3:]
    nh = x_refs[0].shape[1]

    def build(hi, carry):
        for ho in range(nh):
            w = wt_ref[pl.ds(hi * nh + ho, 1), :]
            t = pltpu.roll(jnp.broadcast_to(w, (LANES, 2 * LANES)), 0, 1, stride=1, stride_axis=0)
            mt[pl.ds(pl.multiple_of(hi * LANES, LANES), LANES), LANES * ho:LANES * (ho + 1)] = t[:, :LANES].astype(BF16)
        return carry

    lax.fori_loop(0, nh, build, 0)
    _build_state_table(pw_ref, hb_ref, 2, wct)

    for x_ref, hin_ref, o_ref in zip(x_refs, hin_refs, o_refs):
        def blk(r0, rb, x_ref=x_ref, hin_ref=hin_ref, o_ref=o_ref):
            _gather_chunk_rows(x_ref, vbuf, r0, rb)
            y = jnp.dot(vbuf[0:rb, :].astype(BF16), mt[...], preferred_element_type=F32)
            y = y + lax.dot_general(hin_ref[pl.ds(r0, rb), :].astype(BF16), wct[...], (((1,), (1,)), ((), ())),
                                    preferred_element_type=F32)
            vbuf[0:rb, :] = y
            _scatter_chunk_rows(vbuf, o_ref, r0, rb)

        _for_row_blocks(x_ref.shape[0], vbuf.shape[0], blk)


def s5_main_call(xs, hins, wtoe, pw, hb):
    n_sets = len(xs)
    _, g, h, _ = xs[0].shape
    ns = pw.shape[-1]
    rb = _row_block(xs[0].shape[0])
    tab = lambda i: (i, 0, 0, 0)
    x_spec = lambda x: pl.BlockSpec((x.shape[0], None, h, LANES), lambda i: (0, i, 0, 0))
    return pl.pallas_call(
        functools.partial(_s5_main_body, n_sets=n_sets),
        grid=(g,),
        in_specs=[x_spec(x) for x in xs]
        + [pl.BlockSpec((x.shape[0], ns), lambda i: (0, i)) for x in xs]
        + [pl.BlockSpec((None, h * h, 2 * LANES), lambda i: (i, 0, 0)),
           pl.BlockSpec((None, 4, LANES, ns), tab),
           pl.BlockSpec((None, 4, h, ns), tab)],
        out_specs=[x_spec(x) for x in xs],
        out_shape=[jax.ShapeDtypeStruct(x.shape, F32) for x in xs],
        scratch_shapes=[pltpu.VMEM((h * LANES, h * LANES), BF16), pltpu.VMEM((h * LANES, ns), BF16),
                        pltpu.VMEM((rb, h * LANES), F32)],
        compiler_params=_cparams(("arbitrary",)),
        name="s5_main",
    )(*xs, *hins, wtoe, pw, hb)


def _band(n, w):
    i = np.arange(n)
    return ((i[None, :] >= i[:, None] - w // 2) & (i[None, :] < i[:, None] + w // 2)).astype(np.float32)


def _cnt(n, w):
    i = np.arange(n)
    return (np.clip(i + w // 2, 0, n) - np.clip(i - w // 2, 0, n)).astype(np.float32)


def _pool_grid_body(wv_ref, x_ref, cb_ref, rc_ref, pw_ref, ps_ref, o_ref, cs, *, tb, halo):
    t = x_ref.shape[1]
    wh = wv_ref[pl.program_id(0)] // 2
    zeros = jnp.zeros((halo, LANES), F32)
    cs[0:halo, :] = zeros
    cs[halo + t:halo + t + halo, :] = zeros
    for i in range(t // tb):
        cs[halo + i * tb:halo + (i + 1) * tb, :] = jnp.dot(cb_ref[...], x_ref[0, i * tb:(i + 1) * tb, :],
                                                            preferred_element_type=F32)
    def grid_row(r):
        return cs[pl.ds(pl.multiple_of(halo + r * GRID_W, GRID_W), GRID_W), :]

    run = lax.fori_loop(0, wh - 1, lambda r, acc: acc + grid_row(r), jnp.zeros((GRID_W, LANES), F32))
    rpb = tb // GRID_W

    def blk(i, run):
        rows = []
        for q in range(rpb):
            r = i * rpb + q
            run = run + grid_row(r + wh - 1) - grid_row(r - wh - 1)
            rows.append(run)
        tok = pl.multiple_of(i * tb, tb)
        v = x_ref[0, pl.ds(tok, tb), :].astype(F32)
        z = jnp.concatenate(rows, axis=0) * rc_ref[pl.ds(tok, tb), :] - v
        o = jnp.dot(z.astype(BF16), pw_ref[...], preferred_element_type=F32) * ps_ref[...]
        o_ref[0, pl.ds(tok, tb), :] = o.astype(BF16)
        return run

    lax.fori_loop(0, t // tb, blk, run)


def pool_grid_call(pxp, pw, ps):
    b, t, dp = pxp.shape
    ng = len(POOL_WINDOWS)
    cg = dp // ng
    rows = t // GRID_W
    tb = min(512, t)
    halo = (max(POOL_WINDOWS) // 2 + 1) * GRID_W
    cb = np.stack([np.kron(np.eye(tb // GRID_W, dtype=np.float32), _band(GRID_W, w)) for w in POOL_WINDOWS])
    rc = np.stack([1.0 / (np.repeat(_cnt(rows, w), GRID_W) * np.tile(_cnt(GRID_W, w), rows)) for w in POOL_WINDOWS])
    rc = np.broadcast_to(rc[:, :, None], (ng, t, cg)).astype(np.float32)
    gs = pltpu.PrefetchScalarGridSpec(
        num_scalar_prefetch=1,
        grid=(ng, b),
        in_specs=[pl.BlockSpec((1, t, cg), lambda g, i, wv: (i, 0, g)),
                  pl.BlockSpec((None, tb, tb), lambda g, i, wv: (g, 0, 0)),
                  pl.BlockSpec((None, t, cg), lambda g, i, wv: (g, 0, 0)),
                  pl.BlockSpec((None, cg, cg), lambda g, i, wv: (g, 0, 0)),
                  pl.BlockSpec((1, cg), lambda g, i, wv: (0, g))],
        out_specs=pl.BlockSpec((1, t, cg), lambda g, i, wv: (i, 0, g)),
        scratch_shapes=[pltpu.VMEM((t + 2 * halo, cg), F32)])
    return pl.pallas_call(
        functools.partial(_pool_grid_body, tb=tb, halo=halo),
        grid_spec=gs,
        out_shape=jax.ShapeDtypeStruct((b, t, dp), BF16),
        compiler_params=_cparams(("arbitrary", "arbitrary")),
        name="pool_grid",
    )(jnp.asarray(POOL_WINDOWS, jnp.int32), pxp, jnp.asarray(cb, BF16), jnp.asarray(rc), pw, ps)


def _pool_seq_body(x_ref, cb_ref, rc_ref, pw_ref, ps_ref, o_ref):
    v = x_ref[0]
    m = jnp.dot(cb_ref[...], v, preferred_element_type=F32) * rc_ref[...]
    z = m - v.astype(F32)
    o_ref[0] = (jnp.dot(z.astype(BF16), pw_ref[...], preferred_element_type=F32) * ps_ref[...]).astype(BF16)


def pool_seq_call(pcp, pw, ps):
    b, t, dp = pcp.shape
    ng = len(POOL_WINDOWS)
    cg = dp // ng
    cb = np.stack([_band(t, w) for w in POOL_WINDOWS])
    rc = np.stack([1.0 / _cnt(t, w) for w in POOL_WINDOWS])
    rc = np.broadcast_to(rc[:, :, None], (ng, t, cg)).astype(np.float32)
    return pl.pallas_call(
        _pool_seq_body,
        grid=(ng, b),
        in_specs=[pl.BlockSpec((1, t, cg), lambda g, i: (i, 0, g)),
                  pl.BlockSpec((None, t, t), lambda g, i: (g, 0, 0)),
                  pl.BlockSpec((None, t, cg), lambda g, i: (g, 0, 0)),
                  pl.BlockSpec((None, cg, cg), lambda g, i: (g, 0, 0)),
                  pl.BlockSpec((1, cg), lambda g, i: (0, g))],
        out_specs=pl.BlockSpec((1, t, cg), lambda g, i: (i, 0, g)),
        out_shape=jax.ShapeDtypeStruct((b, t, dp), BF16),
        compiler_params=_cparams(("arbitrary", "arbitrary")),
        name="pool_seq",
    )(pcp, jnp.asarray(cb, BF16), jnp.asarray(rc), pw, ps)


def _mixer_out_body(x_ref, yt_ref, pool_ref, mod_ref, wg_ref, wos_ref, wop_ref, gpost_ref, gpre_ref, wr_ref,
                    x1_ref, hx_ref, aff_ref):
    m = mod_ref[0]
    yt = jnp.concatenate([yt_ref[j] for j in range(yt_ref.shape[0])], axis=1)
    ge = jax.nn.gelu(yt)
    zt = jnp.dot(wg_ref[...], ge.astype(BF16), preferred_element_type=F32)
    s5o = (ge * jax.nn.sigmoid(zt)).astype(BF16)
    mix = lax.dot_general(s5o, wos_ref[...], (((0,), (0,)), ((), ())), preferred_element_type=F32)
    mix = mix + jnp.dot(pool_ref[0], wop_ref[...], preferred_element_type=F32)
    x1 = x_ref[0] + m[2:3] * _rms(mix, gpost_ref[...])
    x1_ref[0] = x1
    h2 = _rms(x1, gpre_ref[...]) * (1.0 + m[4:5]) + m[3:4]
    h_hi = h2.astype(BF16)
    h_lo = (h2 - h_hi.astype(F32)).astype(BF16)
    hx_ref[0] = h_hi
    ne = aff_ref.shape[1]
    nt_dims = (((1,), (1,)), ((), ()))
    t1 = lax.dot_general(wr_ref[...], h_hi, nt_dims, preferred_element_type=F32)
    t2 = lax.dot_general(wr_ref[...], h_lo, nt_dims, preferred_element_type=F32)
    lg = t1[0:ne] + t1[ne:2 * ne] + t2[0:ne]
    e = jnp.exp(lg - jnp.max(lg, axis=0, keepdims=True))
    aff_ref[0] = e / jnp.sum(e, axis=0, keepdims=True)


def router_split(w_router):
    d, ne = w_router.shape
    rows = -(-2 * ne // BF16_ROWS) * BF16_ROWS
    w = jnp.transpose(w_router.astype(F32))
    hi = w.astype(BF16)
    lo = (w - hi.astype(F32)).astype(BF16)
    return jnp.concatenate([hi, lo, jnp.zeros((rows - 2 * ne, d), BF16)], axis=0)


def mixer_out_call(x, yt, pool, mod3, mod_row, wglu_t, wo_s, wo_p, g_post, g_pre, wr2, ne):
    b, t, d = x.shape
    ds, dp = wo_s.shape[0], wo_p.shape[0]
    tm = min(512, t)
    nj = tm // LANES
    mod_idx = (lambda i, j: (i, 0, 0)) if mod_row is None else (lambda i, j: (mod_row, 0, 0))
    const = lambda i, j: (0, 0)
    return pl.pallas_call(
        _mixer_out_body,
        grid=(b, t // tm),
        in_specs=[pl.BlockSpec((1, tm, d), lambda i, j: (i, j, 0)),
                  pl.BlockSpec((nj, None, ds, LANES), lambda i, j: (j, i, 0, 0)),
                  pl.BlockSpec((1, tm, dp), lambda i, j: (i, j, 0)),
                  pl.BlockSpec((1, 6, d), mod_idx),
                  pl.BlockSpec((ds, ds), const),
                  pl.BlockSpec((ds, d), const),
                  pl.BlockSpec((dp, d), const),
                  pl.BlockSpec((1, d), const),
                  pl.BlockSpec((1, d), const),
                  pl.BlockSpec(wr2.shape, const)],
        out_specs=[pl.BlockSpec((1, tm, d), lambda i, j: (i, j, 0)),
                   pl.BlockSpec((1, tm, d), lambda i, j: (i, j, 0)),
                   pl.BlockSpec((1, ne, tm), lambda i, j: (i, 0, j))],
        out_shape=[jax.ShapeDtypeStruct((b, t, d), F32),
                   jax.ShapeDtypeStruct((b, t, d), BF16),
                   jax.ShapeDtypeStruct((b, ne, t), F32)],
        compiler_params=_cparams(("arbitrary", "arbitrary")),
        name="mixer_out",
    )(x, yt, pool, mod3, wglu_t, wo_s, wo_p, g_post.reshape(1, d), g_pre.reshape(1, d), wr2)


def _select_body(aff_ref, tri_ref, pos_ref, offs_ref, *, cap, tile):
    a = aff_ref[0]
    ne, t = a.shape
    bits = pltpu.bitcast(a, jnp.int32)

    def count_ge(thr):
        return jnp.sum((bits >= thr).astype(jnp.int32), axis=1, keepdims=True)

    def bis(_, c):
        lo, hi = c
        mid = lo + lax.shift_right_logical(hi - lo, 1)
        ok = count_ge(mid) >= cap
        return jnp.where(ok, mid, lo), jnp.where(ok, hi, mid)

    lo0 = jnp.zeros((ne, 1), jnp.int32)
    hi0 = jnp.full((ne, 1), 0x7F800001, jnp.int32)
    thr, _ = lax.fori_loop(0, 32, bis, (lo0, hi0))
    gt = bits > thr
    eq = bits == thr
    need = (cap - jnp.sum(gt.astype(jnp.int32), axis=1, keepdims=True)).astype(F32)
    tri = tri_ref[...]
    nblk = t // tile

    def cumsum(mask_f32):
        run = jnp.zeros((ne, 1), F32)
        pieces, starts = [], []
        for j in range(nblk):
            starts.append(run)
            cj = jnp.dot(mask_f32[:, tile * j:tile * (j + 1)].astype(BF16), tri, preferred_element_type=F32) + run
            pieces.append(cj)
            run = cj[:, tile - 1:tile]
        starts.append(run)
        return jnp.concatenate(pieces, axis=1) if nblk > 1 else pieces[0], starts

    ceq, _ = cumsum(jnp.where(eq, 1.0, 0.0))
    sel = gt | (eq & (ceq <= need))
    csel, starts = cumsum(jnp.where(sel, 1.0, 0.0))
    pos_ref[0] = jnp.where(sel, csel.astype(jnp.int32) - 1, -1)
    lane = lax.broadcasted_iota(jnp.int32, (ne, LANES), 1)
    offs = jnp.zeros((ne, LANES), F32)
    for j, s in enumerate(starts):
        offs = jnp.where(lane == j, s, offs)
    offs_ref[0] = offs.astype(jnp.int32)


def select_call(afft, cap, tile):
    b, ne, t = afft.shape
    assert t % tile == 0 and t // tile + 1 <= LANES
    tri = jnp.asarray(np.triu(np.ones((tile, tile), np.float32)), BF16)
    pos, offs = pl.pallas_call(
        functools.partial(_select_body, cap=cap, tile=tile),
        grid=(b,),
        in_specs=[pl.BlockSpec((1, ne, t), lambda i: (i, 0, 0)),
                  pl.BlockSpec((tile, tile), lambda i: (0, 0))],
        out_specs=[pl.BlockSpec((1, ne, t), lambda i: (i, 0, 0)),
                   pl.BlockSpec((1, ne, LANES), lambda i: (i, 0, 0))],
        out_shape=[jax.ShapeDtypeStruct((b, ne, t), jnp.int32),
                   jax.ShapeDtypeStruct((b, ne, LANES), jnp.int32)],
        compiler_params=_cparams(("arbitrary",)),
        name="select",
    )(afft, tri)
    nt = t // tile
    offs_flat = jnp.transpose(offs[:, :, :nt + 1], (0, 2, 1)).reshape(-1)
    return pos, offs_flat


def _align_down(v):
    return lax.shift_left(lax.shift_right_logical(v, 4), 4)


def _tile_slots(offs_ref, b, j, e, nt, ne):
    s0 = offs_ref[(b * (nt + 1) + j) * ne + e]
    s1 = offs_ref[(b * (nt + 1) + j + 1) * ne + e]
    return s0, s1


def _one_hot_t(pos_row, base, win, value=None):
    tt = pos_row.shape[1]
    k = lax.broadcasted_iota(jnp.int32, (win, tt), 0)
    hit = k == (pos_row - base)
    if value is None:
        return jnp.where(hit, 1.0, 0.0).astype(BF16)
    return jnp.where(hit, value, 0.0).astype(BF16)


def _dispatch_body(offs_ref, hx_ref, pos_ref, x_hbm, stage, stage2, carry, sem, sem2, *, nt, ne, win, cap):
    i = pl.program_id(0)
    n_steps = pl.num_programs(0)
    b, j = i // nt, i % nt
    slot = i % 2
    d = hx_ref.shape[2]

    @pl.when(j == 0)
    def _():
        carry[...] = jnp.zeros(carry.shape, carry.dtype)

    hx = hx_ref[0]
    s0s, s1s, a0s = [], [], []
    for e in range(ne):
        s0, s1 = _tile_slots(offs_ref, b, j, e, nt, ne)
        s0s.append(s0)
        s1s.append(s1)
        a0s.append(_align_down(s0))

    def windows(c):
        q = jnp.concatenate([_one_hot_t(pos_ref[0, e:e + 1, :], a0s[e] + c * win, win) for e in range(ne)], axis=0)
        return jnp.dot(q, hx, preferred_element_type=F32)

    def chunk_copy(buf, e, c, semref):
        dst = x_hbm.at[e, b, pl.ds(pl.multiple_of(a0s[e] + c * win, BF16_ROWS), win), :]
        return pltpu.make_async_copy(buf.at[e], dst, semref.at[e])

    def first_chunk_copy(step, e):
        bb, jj = step // nt, step % nt
        a = _align_down(offs_ref[(bb * (nt + 1) + jj) * ne + e])
        sl = step % 2
        dst = x_hbm.at[e, bb, pl.ds(pl.multiple_of(a, BF16_ROWS), win), :]
        return pltpu.make_async_copy(stage.at[sl, e], dst, sem.at[sl, e])

    xw = windows(0).reshape(ne, win, d)
    head = xw[:, :BF16_ROWS, :] + carry[...].astype(F32)
    stage[slot, :, :BF16_ROWS, :] = head.astype(BF16)
    stage[slot, :, BF16_ROWS:, :] = xw[:, BF16_ROWS:, :].astype(BF16)

    @pl.when(i > 0)
    def _():
        for e in range(ne):
            first_chunk_copy(i - 1, e).wait()

    for e in range(ne):
        first_chunk_copy(i, e).start()
        off = _align_down(s1s[e]) - a0s[e]

        @pl.when(off < win)
        def _():
            carry[e] = stage[slot, e, pl.ds(pl.multiple_of(off, BF16_ROWS), BF16_ROWS), :]

    nch = [lax.div(s1s[e] - a0s[e], win) + 1 for e in range(ne)]
    nmax = functools.reduce(jnp.maximum, nch)

    def extra(c, cr):
        xc = windows(c).reshape(ne, win, d)
        stage2[...] = xc.astype(BF16)
        for e in range(ne):
            @pl.when(c < nch[e])
            def _():
                cp = chunk_copy(stage2, e, c, sem2)
                cp.start()
                cp.wait()
                off = _align_down(s1s[e]) - a0s[e] - c * win

                @pl.when(jnp.logical_and(off >= 0, off < win))
                def _():
                    carry[e] = stage2[e, pl.ds(pl.multiple_of(off, BF16_ROWS), BF16_ROWS), :]
        return cr

    lax.fori_loop(1, nmax, extra, 0)

    if nt > 1:
        assert cap >= win

        @pl.when(j == 0)
        def _():
            stage2[0] = jnp.zeros((win, d), BF16)
            tails = [pltpu.make_async_copy(stage2.at[0], x_hbm.at[e, b, pl.ds(cap, win), :], sem2.at[e])
                     for e in range(ne)]
            for cp in tails:
                cp.start()
            for cp in tails:
                cp.wait()

    @pl.when(i == n_steps - 1)
    def _():
        for e in range(ne):
            first_chunk_copy(i, e).wait()


def dispatch_call(hx, pos, offs_flat, cap, tile, win):
    b, t, d = hx.shape
    ne = pos.shape[1]
    nt = t // tile
    cpad = cap + win if nt > 1 else win
    assert nt > 1 or cap < win
    gs = pltpu.PrefetchScalarGridSpec(
        num_scalar_prefetch=1,
        grid=(b * nt,),
        in_specs=[pl.BlockSpec((1, tile, d), lambda i, o: (i // nt, i % nt, 0)),
                  pl.BlockSpec((1, ne, tile), lambda i, o: (i // nt, 0, i % nt))],
        out_specs=pl.BlockSpec(memory_space=pl.ANY),
        scratch_shapes=[pltpu.VMEM((2, ne, win, d), BF16), pltpu.VMEM((ne, win, d), BF16),
                        pltpu.VMEM((ne, BF16_ROWS, d), BF16),
                        pltpu.SemaphoreType.DMA((2, ne)), pltpu.SemaphoreType.DMA((ne,))])
    return pl.pallas_call(
        functools.partial(_dispatch_body, nt=nt, ne=ne, win=win, cap=cap),
        grid_spec=gs,
        out_shape=jax.ShapeDtypeStruct((ne, b, cpad, d), BF16),
        compiler_params=_cparams(("arbitrary",)),
        name="dispatch",
    )(offs_flat, hx, pos)


def _ffn_body(x_ref, wg_ref, wu_ref, wd_ref, y_ref, wgb, wub, wdb, *, rows, rb, batched):
    def cast_weights():
        for src, dst in ((wg_ref, wgb), (wu_ref, wub), (wd_ref, wdb)):
            for r in range(0, src.shape[0], 256):
                dst[r:r + 256, :] = src[r:r + 256, :].astype(BF16)

    if batched:
        pl.when(pl.program_id(1) == 0)(cast_weights)
    else:
        cast_weights()
    for r in range(0, rows, rb):
        xb = x_ref[r:r + rb, :]
        a = jnp.dot(xb, wgb[...], preferred_element_type=F32)
        g = jnp.dot(xb, wub[...], preferred_element_type=F32)
        h = (jax.nn.silu(a) * g).astype(BF16)
        y_ref[r:r + rb, :] = jnp.dot(h, wdb[...], preferred_element_type=F32).astype(BF16)
    if y_ref.shape[0] > rows:
        y_ref[rows:, :] = jnp.zeros((y_ref.shape[0] - rows, y_ref.shape[1]), BF16)


def ffn_call(xs, wg, wu, wd, cap):
    ne, b, cpad, d = xs.shape
    f = wg.shape[2]
    if cap % 256 == 0:
        grid = (ne, b)
        x_spec = pl.BlockSpec((None, None, cap, d), lambda e, i: (e, i, 0, 0))
        y_spec = pl.BlockSpec((None, None, cpad, d), lambda e, i: (e, i, 0, 0))
        wmap = lambda e, i: (e, 0, 0)
        body = functools.partial(_ffn_body, rows=cap, rb=256, batched=True)
        sem = ("arbitrary", "arbitrary")
        args, shape = (xs, wg, wu, wd), xs.shape
    else:
        rows = b * cpad
        grid = (ne,)
        x_spec = pl.BlockSpec((None, rows, d), lambda e: (e, 0, 0))
        y_spec = pl.BlockSpec((None, rows, d), lambda e: (e, 0, 0))
        wmap = lambda e: (e, 0, 0)
        body = functools.partial(_ffn_body, rows=rows, rb=rows, batched=False)
        sem = ("arbitrary",)
        args, shape = (xs.reshape(ne, rows, d), wg, wu, wd), (ne, rows, d)
    y = pl.pallas_call(
        body,
        grid=grid,
        in_specs=[x_spec, pl.BlockSpec((None, d, f), wmap), pl.BlockSpec((None, d, f), wmap),
                  pl.BlockSpec((None, f, d), wmap)],
        out_specs=y_spec,
        out_shape=jax.ShapeDtypeStruct(shape, BF16),
        scratch_shapes=[pltpu.VMEM((d, f), BF16), pltpu.VMEM((d, f), BF16), pltpu.VMEM((f, d), BF16)],
        compiler_params=_cparams(sem),
        name="ffn",
    )(*args)
    return y.reshape(xs.shape)


def _combine_body(offs_ref, x1_ref, pos_ref, aff_ref, mod_ref, g_ref, y_hbm, o_ref, ywin, ywin2, sem, sem2,
                  *, nt, ne, win):
    i = pl.program_id(0)
    n_steps = pl.num_programs(0)
    b, j = i // nt, i % nt
    slot = i % 2
    d = x1_ref.shape[2]
    cpad = y_hbm.shape[2]

    def window_copy(step, e, sl):
        bb, jj = step // nt, step % nt
        a = _align_down(offs_ref[(bb * (nt + 1) + jj) * ne + e])
        return pltpu.make_async_copy(y_hbm.at[e, bb, pl.ds(pl.multiple_of(a, BF16_ROWS), win), :],
                                     ywin.at[sl, e], sem.at[sl, e])

    @pl.when(i == 0)
    def _():
        for e in range(ne):
            window_copy(i, e, slot).start()

    @pl.when(i + 1 < n_steps)
    def _():
        for e in range(ne):
            window_copy(i + 1, e, 1 - slot).start()

    s1s, a0s = [], []
    for e in range(ne):
        s0, s1 = _tile_slots(offs_ref, b, j, e, nt, ne)
        s1s.append(s1)
        a0s.append(_align_down(s0))

    def weights(c):
        return jnp.concatenate(
            [_one_hot_t(pos_ref[0, e:e + 1, :], a0s[e] + c * win, win, aff_ref[0, e:e + 1, :]) for e in range(ne)],
            axis=0)

    pt = weights(0)
    for e in range(ne):
        window_copy(i, e, slot).wait()
    moe = lax.dot_general(pt, ywin[slot].reshape(ne * win, d), (((0,), (0,)), ((), ())), preferred_element_type=F32)

    nch = [lax.div(s1s[e] - a0s[e], win) + 1 for e in range(ne)]
    nmax = functools.reduce(jnp.maximum, nch)

    def extra(c, acc):
        for e in range(ne):
            a = jnp.minimum(a0s[e] + c * win, cpad - win)
            cp = pltpu.make_async_copy(y_hbm.at[e, b, pl.ds(pl.multiple_of(a, BF16_ROWS), win), :],
                                       ywin2.at[e], sem2.at[e])
            cp.start()
            cp.wait()
        return acc + lax.dot_general(weights(c), ywin2[...].reshape(ne * win, d), (((0,), (0,)), ((), ())),
                                     preferred_element_type=F32)

    moe = lax.fori_loop(1, nmax, extra, moe)
    m = mod_ref[0]
    o_ref[0] = x1_ref[0] + m[5:6] * _rms(moe, g_ref[...])


def combine_call(x1, ys, pos, afft, offs_flat, mod3, mod_row, g_post, tile, win):
    b, t, d = x1.shape
    ne = pos.shape[1]
    nt = t // tile
    mod_idx = (lambda i, o: (i // nt, 0, 0)) if mod_row is None else (lambda i, o: (mod_row, 0, 0))
    gs = pltpu.PrefetchScalarGridSpec(
        num_scalar_prefetch=1,
        grid=(b * nt,),
        in_specs=[pl.BlockSpec((1, tile, d), lambda i, o: (i // nt, i % nt, 0)),
                  pl.BlockSpec((1, ne, tile), lambda i, o: (i // nt, 0, i % nt)),
                  pl.BlockSpec((1, ne, tile), lambda i, o: (i // nt, 0, i % nt)),
                  pl.BlockSpec((1, 6, d), mod_idx),
                  pl.BlockSpec((1, d), lambda i, o: (0, 0)),
                  pl.BlockSpec(memory_space=pl.ANY)],
        out_specs=pl.BlockSpec((1, tile, d), lambda i, o: (i // nt, i % nt, 0)),
        scratch_shapes=[pltpu.VMEM((2, ne, win, d), BF16), pltpu.VMEM((ne, win, d), BF16),
                        pltpu.SemaphoreType.DMA((2, ne)), pltpu.SemaphoreType.DMA((ne,))])
    return pl.pallas_call(
        functools.partial(_combine_body, nt=nt, ne=ne, win=win),
        grid_spec=gs,
        out_shape=jax.ShapeDtypeStruct((b, t, d), F32),
        compiler_params=_cparams(("arbitrary",)),
        name="combine",
    )(offs_flat, x1, pos, afft, mod3, g_post.reshape(1, d), ys)


def ec_moe_call(x1, hx, afft, mod3, mod_row, g_post, wg, wu, wd):
    b, t, d = x1.shape
    ne = afft.shape[1]
    cap = EC_CAPACITY_FACTOR * t // ne
    tile = min(256, t)
    win = min(cap, 3 * EC_CAPACITY_FACTOR * tile // (2 * ne)) + BF16_ROWS
    win = -(-win // BF16_ROWS) * BF16_ROWS
    pos, offs_flat = select_call(afft, cap, tile)
    xs = dispatch_call(hx, pos, offs_flat, cap, tile, win)
    ys = ffn_call(xs, wg, wu, wd, cap)
    return combine_call(x1, ys, pos, afft, offs_flat, mod3, mod_row, g_post, tile, win)


def kernel(x, c, ctx, c_ctx, w_mod, b_mod, g_mix_pre, g_mix_post, g_ffn_pre, g_ffn_post, w_in, s5_lam_re, s5_lam_im,
           s5_log_dt, s5_b_re, s5_b_im, s5_c_re, s5_c_im, s5_d, s5_w_glu, pool_w, pool_scale, w_out, w_router,
           w_gate, w_up, w_down):
    depth = w_mod.shape[0]
    b, t, d = x.shape
    _, _, g, p, h = s5_b_re.shape
    ds = g * h
    dp = d - ds
    assert t % min(512, t) == 0 and t % LANES == 0 and t % GRID_W == 0
    assert ctx.shape[1] % LANES == 0 and 2 * p == LANES
    nmod = w_mod.shape[2] // d
    rows = -(-(b + 1) // SUBLANES) * SUBLANES
    cc = jnp.zeros((rows, d), F32).at[:b].set(c.astype(F32)).at[b].set(c_ctx.astype(F32))
    ncl, ncc = t // LANES, ctx.shape[1] // LANES

    for l in range(depth):
        last = l == depth - 1
        mod3 = mod_call(cc, w_mod[l].astype(F32), b_mod[l].astype(F32)).reshape(rows, nmod, d)
        ws_t = jnp.transpose(w_in[l][:, :ds]).astype(BF16)
        wp = w_in[l][:, ds:].astype(BF16)
        wtoe, ptab, htab, dr, di = s5_tables(s5_lam_re[l], s5_lam_im[l], s5_log_dt[l], s5_b_re[l], s5_b_im[l],
                                             s5_c_re[l], s5_c_im[l], s5_d[l])
        pw = pool_w[l].astype(BF16)
        ps = pool_scale[l].astype(F32).reshape(1, dp)
        wglu_t = jnp.transpose(s5_w_glu[l]).astype(BF16)
        wo_s, wo_p = w_out[l][:ds].astype(BF16), w_out[l][ds:].astype(BF16)
        wr2 = router_split(w_router[l])
        ne = w_router.shape[2]
        wg, wu, wd = w_gate[l].astype(F32), w_up[l].astype(F32), w_down[l].astype(F32)

        pxt, pxp = mixer_in_call(x, mod3, None, g_mix_pre[l], ws_t, wp)
        pct, pcp = mixer_in_call(ctx, mod3, b, g_mix_pre[l], ws_t, wp)
        xl, xc = pxt.reshape(ncl * b, g, h, LANES), pct.reshape(ncc * b, g, h, LANES)
        sl, sc = s5_states_call(xl, xc, ptab, htab)
        hl, hc = s5_recur_call(sl.reshape(ncl, b, -1), sc.reshape(ncc, b, -1), dr, di, p)
        hl, hc = hl.reshape(ncl * b, -1), hc.reshape(ncc * b, -1)
        if last:
            (yt,) = s5_main_call([xl], [hl], wtoe, ptab, htab)
        else:
            yt, yc = s5_main_call([xl, xc], [hl, hc], wtoe, ptab, htab)
        pool_x = pool_grid_call(pxp, pw, ps)
        x1, hx, afft = mixer_out_call(x, yt.reshape(ncl, b, ds, LANES), pool_x, mod3, None, wglu_t, wo_s, wo_p,
                                      g_mix_post[l], g_ffn_pre[l], wr2, ne)
        x = ec_moe_call(x1, hx, afft, mod3, None, g_ffn_post[l], wg, wu, wd)
        if not last:
            pool_c = pool_seq_call(pcp, pw, ps)
            c1, hcx, affc = mixer_out_call(ctx, yc.reshape(ncc, b, ds, LANES), pool_c, mod3, b, wglu_t, wo_s, wo_p,
                                           g_mix_post[l], g_ffn_pre[l], wr2, ne)
            ctx = ec_moe_call(c1, hcx, affc, mod3, b, g_ffn_post[l], wg, wu, wd)
    return x
```

```python
import functools
import math

import numpy as np
import jax
import jax.numpy as jnp
from jax import lax
from jax.experimental import pallas as pl
from jax.experimental.pallas import tpu as pltpu

GRID_W = 64
POOL_WINDOWS = (2, 4, 8, 16)
EC_CAPACITY_FACTOR = 2
RMS_EPS = 1e-6

LANES = 128
CHUNK = LANES // 2
SUBLANES = 8
BF16_ROWS = 16
VMEM_LIMIT = 56 * 1024 * 1024

F32 = jnp.float32
BF16 = jnp.bfloat16
HI = lax.Precision.HIGHEST


def _cparams(sem):
    return pltpu.CompilerParams(dimension_semantics=sem, vmem_limit_bytes=VMEM_LIMIT)


def _rms(v, g):
    return v * lax.rsqrt(jnp.mean(v * v, axis=-1, keepdims=True) + RMS_EPS) * g


def _mod_body(c_ref, w_ref, b_ref, o_ref):
    s = jax.nn.silu(c_ref[...])
    o_ref[...] = jnp.dot(s, w_ref[...], preferred_element_type=F32, precision=HI) + b_ref[...]


def mod_call(cc, wm, bm):
    r, d = cc.shape
    n = wm.shape[1]
    return pl.pallas_call(
        _mod_body,
        grid=(n // d,),
        in_specs=[pl.BlockSpec((r, d), lambda j: (0, 0)),
                  pl.BlockSpec((d, d), lambda j: (0, j)),
                  pl.BlockSpec((1, d), lambda j: (0, j))],
        out_specs=pl.BlockSpec((r, d), lambda j: (0, j)),
        out_shape=jax.ShapeDtypeStruct((r, n), F32),
        compiler_params=_cparams(("arbitrary",)),
        name="mod",
    )(cc, wm, bm.reshape(1, n))


def _mixer_in_body(x_ref, mod_ref, g_ref, ws_ref, wp_ref, pxt_ref, pxp_ref):
    m = mod_ref[0]
    h = _rms(x_ref[0], g_ref[...]) * (1.0 + m[1:2]) + m[0:1]
    hb = h.astype(BF16)
    pxp_ref[0] = jnp.dot(hb, wp_ref[...], preferred_element_type=F32).astype(BF16)
    pt = lax.dot_general(ws_ref[...], hb, (((1,), (1,)), ((), ())), preferred_element_type=F32)
    for j in range(pxt_ref.shape[0]):
        pxt_ref[j] = pt[:, LANES * j:LANES * (j + 1)]


def mixer_in_call(x, mod3, mod_row, g, ws_t, wp):
    b, t, d = x.shape
    ds, dp = ws_t.shape[0], wp.shape[1]
    tm = min(512, t)
    nj = tm // LANES
    mod_idx = (lambda i, j: (i, 0, 0)) if mod_row is None else (lambda i, j: (mod_row, 0, 0))
    return pl.pallas_call(
        _mixer_in_body,
        grid=(b, t // tm),
        in_specs=[pl.BlockSpec((1, tm, d), lambda i, j: (i, j, 0)),
                  pl.BlockSpec((1, 6, d), mod_idx),
                  pl.BlockSpec((1, d), lambda i, j: (0, 0)),
                  pl.BlockSpec((ds, d), lambda i, j: (0, 0)),
                  pl.BlockSpec((d, dp), lambda i, j: (0, 0))],
        out_specs=[pl.BlockSpec((nj, None, ds, LANES), lambda i, j: (j, i, 0, 0)),
                   pl.BlockSpec((1, tm, dp), lambda i, j: (i, j, 0))],
        out_shape=[jax.ShapeDtypeStruct((t // LANES, b, ds, LANES), F32),
                   jax.ShapeDtypeStruct((b, t, dp), BF16)],
        compiler_params=_cparams(("arbitrary", "arbitrary")),
        name="mixer_in",
    )(x, mod3, g.reshape(1, d), ws_t, wp)


def s5_tables(lam_re, lam_im, log_dt, b_re, b_im, c_re, c_im, d_skip):
    n_dir, g, p = lam_re.shape
    h = b_re.shape[-1]
    lc = CHUNK
    lam_re, lam_im = lam_re.astype(F32), lam_im.astype(F32)
    dt = jnp.exp(log_dt.astype(F32))[..., None]
    zr, zi = lam_re * dt, lam_im * dt
    er = jnp.exp(zr)
    ar, ai = er * jnp.cos(zi), er * jnp.sin(zi)
    den = lam_re * lam_re + lam_im * lam_im
    qr = ((ar - 1.0) * lam_re + ai * lam_im) / den
    qi = (ai * lam_re - (ar - 1.0) * lam_im) / den
    br, bi = b_re.astype(F32), b_im.astype(F32)
    bbr = qr[..., None] * br - qi[..., None] * bi
    bbi = qr[..., None] * bi + qi[..., None] * br
    cr, ci = c_re.astype(F32), c_im.astype(F32)
    jj = jnp.arange(lc + 1, dtype=F32)
    pe = jnp.exp(zr[..., None] * jj)
    pr, pi = pe * jnp.cos(zi[..., None] * jj), pe * jnp.sin(zi[..., None] * jj)
    e_r = jnp.einsum('dgop,dgpi->dgpoi', cr, bbr, precision=HI) - jnp.einsum('dgop,dgpi->dgpoi', ci, bbi, precision=HI)
    e_i = jnp.einsum('dgop,dgpi->dgpoi', cr, bbi, precision=HI) + jnp.einsum('dgop,dgpi->dgpoi', ci, bbr, precision=HI)
    kk = (jnp.einsum('dgpj,dgpoi->dgjoi', pr[..., :lc], e_r, precision=HI)
          - jnp.einsum('dgpj,dgpoi->dgjoi', pi[..., :lc], e_i, precision=HI))
    kf = jnp.transpose(kk[0], (0, 3, 2, 1))
    kr = jnp.transpose(kk[1], (0, 3, 2, 1))
    dsk = d_skip.astype(F32).reshape(g, h)
    eye = jnp.eye(h, dtype=F32)[None] * dsk[:, :, None]
    w0 = kf[..., 0] + kr[..., 0] + eye
    wtoe = jnp.concatenate([w0[..., None], kf[..., 1:], jnp.zeros_like(w0)[..., None], kr[..., :0:-1]], axis=-1)
    odd_ho = (jnp.arange(h) % 2 == 1)[None, None, :, None]
    wtoe = jnp.where(odd_ho, jnp.roll(wtoe, lc, axis=-1), wtoe).reshape(g, h * h, 2 * lc)
    tp = lambda a: jnp.swapaxes(a, -1, -2)
    lanes = lambda parts: jnp.concatenate(parts, axis=-1)
    prf, pif, prr, pir = tp(pr[0]), tp(pi[0]), tp(pr[1]), tp(pi[1])
    q_f_r, q_f_i = prf[:, lc - 1::-1][:, :lc], pif[:, lc - 1::-1][:, :lc]
    q_r_r, q_r_i = prr[:, :lc], pir[:, :lc]
    i_f_r, i_f_i = prf[:, 1:lc + 1], pif[:, 1:lc + 1]
    i_r_r, i_r_i = prr[:, lc:0:-1], pir[:, lc:0:-1]
    pw = jnp.stack([lanes([q_f_r, q_f_r, q_r_r, q_r_r]), lanes([q_f_i, q_f_i, q_r_i, q_r_i]),
                    lanes([i_f_r, i_f_i, i_r_r, i_r_i]), lanes([i_f_i, i_f_r, i_r_i, i_r_r])], axis=1)
    b_r, b_i = tp(bbr), tp(bbi)
    hb = jnp.stack([lanes([b_r[0], b_i[0], b_r[1], b_i[1]]), lanes([-b_i[0], b_r[0], -b_i[1], b_r[1]]),
                    lanes([cr[0], -cr[0], cr[1], -cr[1]]), lanes([-ci[0], -ci[0], -ci[1], -ci[1]])], axis=1)
    dre = jnp.stack([pr[0][..., lc], pr[0][..., lc], pr[1][..., lc], pr[1][..., lc]], axis=1)
    dim_ = jnp.stack([-pi[0][..., lc], pi[0][..., lc], -pi[1][..., lc], pi[1][..., lc]], axis=1)
    return wtoe, pw, hb, dre.reshape(1, g * 4 * p), dim_.reshape(1, g * 4 * p)


def _low_half(shape):
    return lax.broadcasted_iota(jnp.int32, shape, len(shape) - 1) < CHUNK


def _gather_chunk_rows(x_ref, vbuf, r0, rows):
    nh = x_ref.shape[1]
    lo = _low_half((SUBLANES, LANES))
    for c0 in range(0, rows, SUBLANES):
        for m in range(nh // 2):
            a = x_ref[pl.ds(r0 + c0, SUBLANES), 2 * m, :]
            b = x_ref[pl.ds(r0 + c0, SUBLANES), 2 * m + 1, :]
            vbuf[0, c0:c0 + SUBLANES, LANES * m:LANES * (m + 1)] = jnp.where(lo, a, pltpu.roll(b, CHUNK, 1))
            vbuf[1, c0:c0 + SUBLANES, LANES * m:LANES * (m + 1)] = jnp.where(lo, pltpu.roll(a, CHUNK, 1), b)


def _scatter_chunk_rows(vbuf, o_ref, r0, rows):
    nh = o_ref.shape[1]
    lo = _low_half((SUBLANES, LANES))
    for c0 in range(0, rows, SUBLANES):
        for m in range(nh // 2):
            p0 = vbuf[0, c0:c0 + SUBLANES, LANES * m:LANES * (m + 1)]
            p1 = vbuf[1, c0:c0 + SUBLANES, LANES * m:LANES * (m + 1)]
            o_ref[pl.ds(r0 + c0, SUBLANES), 2 * m, :] = jnp.where(lo, p0, pltpu.roll(p1, CHUNK, 1))
            o_ref[pl.ds(r0 + c0, SUBLANES), 2 * m + 1, :] = jnp.where(lo, pltpu.roll(p0, CHUNK, 1), p1)


def _row_block(nc):
    for rb in (256, 128, 64, 32, 16, 8):
        if nc % rb == 0:
            return rb
    raise ValueError(f"chunk-row count {nc} must be a multiple of {SUBLANES}")


def _build_state_table(pw_ref, hb_ref, k, tab):
    nh = hb_ref.shape[1]
    pa, pb = pw_ref[k], pw_ref[k + 1]
    for h in range(nh):
        tab[CHUNK * h:CHUNK * (h + 1), :] = (hb_ref[k, h:h + 1, :] * pa + hb_ref[k + 1, h:h + 1, :] * pb).astype(BF16)


def _for_row_blocks(nc, rb_max, fn):
    rb = min(rb_max, nc)
    assert nc % rb == 0

    def blk(i, carry):
        fn(pl.multiple_of(i * rb, rb), rb)
        return carry

    lax.fori_loop(0, nc // rb, blk, 0)


N_HALF = LANES // CHUNK


def _s5_states_body(xl_ref, xc_ref, pw_ref, hb_ref, ol_ref, oc_ref, wb, vbuf):
    _build_state_table(pw_ref, hb_ref, 0, wb)
    for x_ref, o_ref in ((xl_ref, ol_ref), (xc_ref, oc_ref)):
        def blk(r0, rb, x_ref=x_ref, o_ref=o_ref):
            _gather_chunk_rows(x_ref, vbuf, r0, rb)
            for q in range(N_HALF):
                o_ref[q, pl.ds(r0, rb), :] = jnp.dot(vbuf[q, 0:rb, :].astype(BF16), wb[...],
                                                     preferred_element_type=F32)

        _for_row_blocks(x_ref.shape[0], vbuf.shape[1], blk)


def s5_states_call(pxt, pct, pw, hb):
    ncl, g, h, _ = pxt.shape
    ncc = pct.shape[0]
    ns = pw.shape[-1]
    rb = _row_block(ncl)
    tab = lambda i: (i, 0, 0, 0)
    return pl.pallas_call(
        _s5_states_body,
        grid=(g,),
        in_specs=[pl.BlockSpec((ncl, None, h, LANES), lambda i: (0, i, 0, 0)),
                  pl.BlockSpec((ncc, None, h, LANES), lambda i: (0, i, 0, 0)),
                  pl.BlockSpec((None, 4, CHUNK, ns), tab),
                  pl.BlockSpec((None, 4, h, ns), tab)],
        out_specs=[pl.BlockSpec((N_HALF, ncl, ns), lambda i: (0, 0, i)),
                   pl.BlockSpec((N_HALF, ncc, ns), lambda i: (0, 0, i))],
        out_shape=[jax.ShapeDtypeStruct((N_HALF, ncl, g * ns), F32), jax.ShapeDtypeStruct((N_HALF, ncc, g * ns), F32)],
        scratch_shapes=[pltpu.VMEM((h * CHUNK, ns), BF16), pltpu.VMEM((N_HALF, rb, h * CHUNK), F32)],
        compiler_params=_cparams(("arbitrary",)),
        name="s5_states",
    )(pxt, pct, pw, hb)


def _s5_recur_body(sl_ref, sc_ref, dr_ref, di_ref, hl_ref, hc_ref, *, half):
    ncl, ncc = sl_ref.shape[1], sc_ref.shape[1]
    shp = sl_ref.shape[2:]
    lane = lax.broadcasted_iota(jnp.int32, shp, 1)
    is_re = (lane % (2 * half)) < half
    is_fwd = (lane % (4 * half)) < 2 * half
    dr = jnp.broadcast_to(dr_ref[...], shp)
    di = jnp.broadcast_to(di_ref[...], shp)
    nl = shp[1]

    def swap(v):
        return jnp.where(is_re, pltpu.roll(v, nl - half, 1), pltpu.roll(v, half, 1))

    def step(hs, s):
        h, hsw = hs
        return dr * h + di * hsw + s, dr * hsw - di * h + swap(s)

    zero = (jnp.zeros(shp, F32), jnp.zeros(shp, F32))

    def fwd(s_ref, h_ref):
        def body(k, hs):
            for q in range(N_HALF):
                h_ref[q, k] = hs[0]
                hs = step(hs, s_ref[q, k])
            return hs
        return body

    hs = lax.fori_loop(0, ncc, fwd(sc_ref, hc_ref), zero)
    lax.fori_loop(0, ncl, fwd(sl_ref, hl_ref), hs)

    def rev(s_ref, h_ref, n):
        def body(i, hs):
            k = n - 1 - i
            for q in reversed(range(N_HALF)):
                h_ref[q, k] = jnp.where(is_fwd, h_ref[q, k], hs[0])
                hs = step(hs, s_ref[q, k])
            return hs
        return body

    hs = lax.fori_loop(0, ncc, rev(sc_ref, hc_ref, ncc), zero)
    lax.fori_loop(0, ncl, rev(sl_ref, hl_ref, ncl), hs)


def s5_recur_call(sl, sc, dr, di, half):
    _, ncl, b, s = sl.shape
    ncc = sc.shape[1]
    lb = 1024 if s % 1024 == 0 else s
    blk = lambda n: pl.BlockSpec((N_HALF, n, b, lb), lambda i: (0, 0, 0, i))
    return pl.pallas_call(
        functools.partial(_s5_recur_body, half=half),
        grid=(s // lb,),
        in_specs=[blk(ncl), blk(ncc),
                  pl.BlockSpec((1, lb), lambda i: (0, i)),
                  pl.BlockSpec((1, lb), lambda i: (0, i))],
        out_specs=[blk(ncl), blk(ncc)],
        out_shape=[jax.ShapeDtypeStruct(sl.shape, F32), jax.ShapeDtypeStruct(sc.shape, F32)],
        compiler_params=_cparams(("arbitrary",)),
        name="s5_recur",
    )(sl, sc, dr, di)


def _s5_main_body(*refs, n_sets):
    x_refs, hin_refs = refs[:n_sets], refs[n_sets:2 * n_sets]
    wt_ref, pw_ref, hb_ref = refs[2 * n_sets:2 * n_sets + 3]
    o_refs = refs[2 * n_sets + 3:3 * n_sets + 3]
    mt, wct, vbuf = refs[3 * n_sets + 3:]
    nh = x_refs[0].shape[1]
    lo = _low_half((CHUNK, LANES))

    def build(hi, carry):
        for m in range(nh // 2):
            we = wt_ref[pl.ds(hi * nh + 2 * m, 1), :]
            wo = wt_ref[pl.ds(hi * nh + 2 * m + 1, 1), :]
            te = pltpu.roll(jnp.broadcast_to(we, (CHUNK, LANES)), 0, 1, stride=1, stride_axis=0)
            to = pltpu.roll(jnp.broadcast_to(wo, (CHUNK, LANES)), 0, 1, stride=1, stride_axis=0)
            mt[pl.ds(pl.multiple_of(hi * CHUNK, CHUNK), CHUNK), LANES * m:LANES * (m + 1)] = (
                jnp.where(lo, te, to).astype(BF16))
        return carry

    lax.fori_loop(0, nh, build, 0)
    _build_state_table(pw_ref, hb_ref, 2, wct)

    for x_ref, hin_ref, o_ref in zip(x_refs, hin_refs, o_refs):
        def blk(r0, rb, x_ref=x_ref, hin_ref=hin_ref, o_ref=o_ref):
            _gather_chunk_rows(x_ref, vbuf, r0, rb)
            for q in range(N_HALF):
                y = jnp.dot(vbuf[q, 0:rb, :].astype(BF16), mt[...], preferred_element_type=F32)
                y = y + lax.dot_general(hin_ref[q, pl.ds(r0, rb), :].astype(BF16), wct[...],
                                        (((1,), (1,)), ((), ())), preferred_element_type=F32)
                vbuf[q, 0:rb, :] = y
            _scatter_chunk_rows(vbuf, o_ref, r0, rb)

        _for_row_blocks(x_ref.shape[0], vbuf.shape[1], blk)


def s5_main_call(xs, hins, wtoe, pw, hb):
    n_sets = len(xs)
    _, g, h, _ = xs[0].shape
    ns = pw.shape[-1]
    rb = _row_block(xs[0].shape[0])
    tab = lambda i: (i, 0, 0, 0)
    x_spec = lambda x: pl.BlockSpec((x.shape[0], None, h, LANES), lambda i: (0, i, 0, 0))
    return pl.pallas_call(
        functools.partial(_s5_main_body, n_sets=n_sets),
        grid=(g,),
        in_specs=[x_spec(x) for x in xs]
        + [pl.BlockSpec((N_HALF, x.shape[0], ns), lambda i: (0, 0, i)) for x in xs]
        + [pl.BlockSpec((None, h * h, 2 * CHUNK), lambda i: (i, 0, 0)),
           pl.BlockSpec((None, 4, CHUNK, ns), tab),
           pl.BlockSpec((None, 4, h, ns), tab)],
        out_specs=[x_spec(x) for x in xs],
        out_shape=[jax.ShapeDtypeStruct(x.shape, F32) for x in xs],
        scratch_shapes=[pltpu.VMEM((h * CHUNK, h * CHUNK), BF16), pltpu.VMEM((h * CHUNK, ns), BF16),
                        pltpu.VMEM((N_HALF, rb, h * CHUNK), F32)],
        compiler_params=_cparams(("arbitrary",)),
        name="s5_main",
    )(*xs, *hins, wtoe, pw, hb)


def _band(n, w):
    i = np.arange(n)
    return ((i[None, :] >= i[:, None] - w // 2) & (i[None, :] < i[:, None] + w // 2)).astype(np.float32)


def _cnt(n, w):
    i = np.arange(n)
    return (np.clip(i + w // 2, 0, n) - np.clip(i - w // 2, 0, n)).astype(np.float32)


def _pool_grid_body(wv_ref, x_ref, cb_ref, rc_ref, pw_ref, ps_ref, o_ref, cs, *, tb, halo):
    t = x_ref.shape[1]
    wh = wv_ref[pl.program_id(0)] // 2
    zeros = jnp.zeros((halo, LANES), F32)
    cs[0:halo, :] = zeros
    cs[halo + t:halo + t + halo, :] = zeros
    for i in range(t // tb):
        cs[halo + i * tb:halo + (i + 1) * tb, :] = jnp.dot(cb_ref[...], x_ref[0, i * tb:(i + 1) * tb, :],
                                                            preferred_element_type=F32)
    def grid_row(r):
        return cs[pl.ds(pl.multiple_of(halo + r * GRID_W, GRID_W), GRID_W), :]

    run = lax.fori_loop(0, wh - 1, lambda r, acc: acc + grid_row(r), jnp.zeros((GRID_W, LANES), F32))
    rpb = tb // GRID_W

    def blk(i, run):
        rows = []
        for q in range(rpb):
            r = i * rpb + q
            run = run + grid_row(r + wh - 1) - grid_row(r - wh - 1)
            rows.append(run)
        tok = pl.multiple_of(i * tb, tb)
        v = x_ref[0, pl.ds(tok, tb), :].astype(F32)
        z = jnp.concatenate(rows, axis=0) * rc_ref[pl.ds(tok, tb), :] - v
        o = jnp.dot(z.astype(BF16), pw_ref[...], preferred_element_type=F32) * ps_ref[...]
        o_ref[0, pl.ds(tok, tb), :] = o.astype(BF16)
        return run

    lax.fori_loop(0, t // tb, blk, run)


def pool_grid_call(pxp, pw, ps):
    b, t, dp = pxp.shape
    ng = len(POOL_WINDOWS)
    cg = dp // ng
    rows = t // GRID_W
    tb = min(512, t)
    halo = (max(POOL_WINDOWS) // 2 + 1) * GRID_W
    cb = np.stack([np.kron(np.eye(tb // GRID_W, dtype=np.float32), _band(GRID_W, w)) for w in POOL_WINDOWS])
    rc = np.stack([1.0 / (np.repeat(_cnt(rows, w), GRID_W) * np.tile(_cnt(GRID_W, w), rows)) for w in POOL_WINDOWS])
    rc = np.broadcast_to(rc[:, :, None], (ng, t, cg)).astype(np.float32)
    gs = pltpu.PrefetchScalarGridSpec(
        num_scalar_prefetch=1,
        grid=(ng, b),
        in_specs=[pl.BlockSpec((1, t, cg), lambda g, i, wv: (i, 0, g)),
                  pl.BlockSpec((None, tb, tb), lambda g, i, wv: (g, 0, 0)),
                  pl.BlockSpec((None, t, cg), lambda g, i, wv: (g, 0, 0)),
                  pl.BlockSpec((None, cg, cg), lambda g, i, wv: (g, 0, 0)),
                  pl.BlockSpec((1, cg), lambda g, i, wv: (0, g))],
        out_specs=pl.BlockSpec((1, t, cg), lambda g, i, wv: (i, 0, g)),
        scratch_shapes=[pltpu.VMEM((t + 2 * halo, cg), F32)])
    return pl.pallas_call(
        functools.partial(_pool_grid_body, tb=tb, halo=halo),
        grid_spec=gs,
        out_shape=jax.ShapeDtypeStruct((b, t, dp), BF16),
        compiler_params=_cparams(("arbitrary", "arbitrary")),
        name="pool_grid",
    )(jnp.asarray(POOL_WINDOWS, jnp.int32), pxp, jnp.asarray(cb, BF16), jnp.asarray(rc), pw, ps)


def _pool_seq_body(x_ref, cb_ref, rc_ref, pw_ref, ps_ref, o_ref):
    v = x_ref[0]
    m = jnp.dot(cb_ref[...], v, preferred_element_type=F32) * rc_ref[...]
    z = m - v.astype(F32)
    o_ref[0] = (jnp.dot(z.astype(BF16), pw_ref[...], preferred_element_type=F32) * ps_ref[...]).astype(BF16)


def pool_seq_call(pcp, pw, ps):
    b, t, dp = pcp.shape
    ng = len(POOL_WINDOWS)
    cg = dp // ng
    cb = np.stack([_band(t, w) for w in POOL_WINDOWS])
    rc = np.stack([1.0 / _cnt(t, w) for w in POOL_WINDOWS])
    rc = np.broadcast_to(rc[:, :, None], (ng, t, cg)).astype(np.float32)
    return pl.pallas_call(
        _pool_seq_body,
        grid=(ng, b),
        in_specs=[pl.BlockSpec((1, t, cg), lambda g, i: (i, 0, g)),
                  pl.BlockSpec((None, t, t), lambda g, i: (g, 0, 0)),
                  pl.BlockSpec((None, t, cg), lambda g, i: (g, 0, 0)),
                  pl.BlockSpec((None, cg, cg), lambda g, i: (g, 0, 0)),
                  pl.BlockSpec((1, cg), lambda g, i: (0, g))],
        out_specs=pl.BlockSpec((1, t, cg), lambda g, i: (i, 0, g)),
        out_shape=jax.ShapeDtypeStruct((b, t, dp), BF16),
        compiler_params=_cparams(("arbitrary", "arbitrary")),
        name="pool_seq",
    )(pcp, jnp.asarray(cb, BF16), jnp.asarray(rc), pw, ps)


def _mixer_out_body(x_ref, yt_ref, pool_ref, mod_ref, wg_ref, wos_ref, wop_ref, gpost_ref, gpre_ref, wr_ref,
                    x1_ref, hx_ref, aff_ref):
    m = mod_ref[0]
    yt = jnp.concatenate([yt_ref[j] for j in range(yt_ref.shape[0])], axis=1)
    ge = jax.nn.gelu(yt)
    zt = jnp.dot(wg_ref[...], ge.astype(BF16), preferred_element_type=F32)
    s5o = (ge * jax.nn.sigmoid(zt)).astype(BF16)
    mix = lax.dot_general(s5o, wos_ref[...], (((0,), (0,)), ((), ())), preferred_element_type=F32)
    mix = mix + jnp.dot(pool_ref[0], wop_ref[...], preferred_element_type=F32)
    x1 = x_ref[0] + m[2:3] * _rms(mix, gpost_ref[...])
    x1_ref[0] = x1
    h2 = _rms(x1, gpre_ref[...]) * (1.0 + m[4:5]) + m[3:4]
    h_hi = h2.astype(BF16)
    h_lo = (h2 - h_hi.astype(F32)).astype(BF16)
    hx_ref[0] = h_hi
    ne = aff_ref.shape[1]
    nt_dims = (((1,), (1,)), ((), ()))
    t1 = lax.dot_general(wr_ref[...], h_hi, nt_dims, preferred_element_type=F32)
    t2 = lax.dot_general(wr_ref[...], h_lo, nt_dims, preferred_element_type=F32)
    lg = t1[0:ne] + t1[ne:2 * ne] + t2[0:ne]
    e = jnp.exp(lg - jnp.max(lg, axis=0, keepdims=True))
    aff_ref[0] = e / jnp.sum(e, axis=0, keepdims=True)


def router_split(w_router):
    d, ne = w_router.shape
    rows = -(-2 * ne // BF16_ROWS) * BF16_ROWS
    w = jnp.transpose(w_router.astype(F32))
    hi = w.astype(BF16)
    lo = (w - hi.astype(F32)).astype(BF16)
    return jnp.concatenate([hi, lo, jnp.zeros((rows - 2 * ne, d), BF16)], axis=0)


def mixer_out_call(x, yt, pool, mod3, mod_row, wglu_t, wo_s, wo_p, g_post, g_pre, wr2, ne):
    b, t, d = x.shape
    ds, dp = wo_s.shape[0], wo_p.shape[0]
    tm = min(512, t)
    nj = tm // LANES
    mod_idx = (lambda i, j: (i, 0, 0)) if mod_row is None else (lambda i, j: (mod_row, 0, 0))
    const = lambda i, j: (0, 0)
    return pl.pallas_call(
        _mixer_out_body,
        grid=(b, t // tm),
        in_specs=[pl.BlockSpec((1, tm, d), lambda i, j: (i, j, 0)),
                  pl.BlockSpec((nj, None, ds, LANES), lambda i, j: (j, i, 0, 0)),
                  pl.BlockSpec((1, tm, dp), lambda i, j: (i, j, 0)),
                  pl.BlockSpec((1, 6, d), mod_idx),
                  pl.BlockSpec((ds, ds), const),
                  pl.BlockSpec((ds, d), const),
                  pl.BlockSpec((dp, d), const),
                  pl.BlockSpec((1, d), const),
                  pl.BlockSpec((1, d), const),
                  pl.BlockSpec(wr2.shape, const)],
        out_specs=[pl.BlockSpec((1, tm, d), lambda i, j: (i, j, 0)),
                   pl.BlockSpec((1, tm, d), lambda i, j: (i, j, 0)),
                   pl.BlockSpec((1, ne, tm), lambda i, j: (i, 0, j))],
        out_shape=[jax.ShapeDtypeStruct((b, t, d), F32),
                   jax.ShapeDtypeStruct((b, t, d), BF16),
                   jax.ShapeDtypeStruct((b, ne, t), F32)],
        compiler_params=_cparams(("arbitrary", "arbitrary")),
        name="mixer_out",
    )(x, yt, pool, mod3, wglu_t, wo_s, wo_p, g_post.reshape(1, d), g_pre.reshape(1, d), wr2)


def _select_body(aff_ref, tri_ref, pos_ref, offs_ref, *, cap, tile):
    a = aff_ref[0]
    ne, t = a.shape
    bits = pltpu.bitcast(a, jnp.int32)

    def count_ge(thr):
        return jnp.sum((bits >= thr).astype(jnp.int32), axis=1, keepdims=True)

    def bis(_, c):
        lo, hi = c
        mid = lo + lax.shift_right_logical(hi - lo, 1)
        ok = count_ge(mid) >= cap
        return jnp.where(ok, mid, lo), jnp.where(ok, hi, mid)

    lo0 = jnp.zeros((ne, 1), jnp.int32)
    hi0 = jnp.full((ne, 1), 0x7F800001, jnp.int32)
    thr, _ = lax.fori_loop(0, 32, bis, (lo0, hi0))
    gt = bits > thr
    eq = bits == thr
    need = (cap - jnp.sum(gt.astype(jnp.int32), axis=1, keepdims=True)).astype(F32)
    tri = tri_ref[...]
    nblk = t // tile

    def cumsum(mask_f32):
        run = jnp.zeros((ne, 1), F32)
        pieces, starts = [], []
        for j in range(nblk):
            starts.append(run)
            cj = jnp.dot(mask_f32[:, tile * j:tile * (j + 1)].astype(BF16), tri, preferred_element_type=F32) + run
            pieces.append(cj)
            run = cj[:, tile - 1:tile]
        starts.append(run)
        return jnp.concatenate(pieces, axis=1) if nblk > 1 else pieces[0], starts

    ceq, _ = cumsum(jnp.where(eq, 1.0, 0.0))
    sel = gt | (eq & (ceq <= need))
    csel, starts = cumsum(jnp.where(sel, 1.0, 0.0))
    pos_ref[0] = jnp.where(sel, csel.astype(jnp.int32) - 1, -1)
    lane = lax.broadcasted_iota(jnp.int32, (ne, LANES), 1)
    offs = jnp.zeros((ne, LANES), F32)
    for j, s in enumerate(starts):
        offs = jnp.where(lane == j, s, offs)
    offs_ref[0] = offs.astype(jnp.int32)


def select_call(afft, cap, tile):
    b, ne, t = afft.shape
    assert t % tile == 0 and t // tile + 1 <= LANES
    tri = jnp.asarray(np.triu(np.ones((tile, tile), np.float32)), BF16)
    pos, offs = pl.pallas_call(
        functools.partial(_select_body, cap=cap, tile=tile),
        grid=(b,),
        in_specs=[pl.BlockSpec((1, ne, t), lambda i: (i, 0, 0)),
                  pl.BlockSpec((tile, tile), lambda i: (0, 0))],
        out_specs=[pl.BlockSpec((1, ne, t), lambda i: (i, 0, 0)),
                   pl.BlockSpec((1, ne, LANES), lambda i: (i, 0, 0))],
        out_shape=[jax.ShapeDtypeStruct((b, ne, t), jnp.int32),
                   jax.ShapeDtypeStruct((b, ne, LANES), jnp.int32)],
        compiler_params=_cparams(("arbitrary",)),
        name="select",
    )(afft, tri)
    nt = t // tile
    offs_flat = jnp.transpose(offs[:, :, :nt + 1], (0, 2, 1)).reshape(-1)
    return pos, offs_flat


def _align_down(v):
    return lax.shift_left(lax.shift_right_logical(v, 4), 4)


def _tile_slots(offs_ref, b, j, e, nt, ne):
    s0 = offs_ref[(b * (nt + 1) + j) * ne + e]
    s1 = offs_ref[(b * (nt + 1) + j + 1) * ne + e]
    return s0, s1


def _one_hot_t(pos_row, base, win, value=None):
    tt = pos_row.shape[1]
    k = lax.broadcasted_iota(jnp.int32, (win, tt), 0)
    hit = k == (pos_row - base)
    if value is None:
        return jnp.where(hit, 1.0, 0.0).astype(BF16)
    return jnp.where(hit, value, 0.0).astype(BF16)


def _dispatch_body(offs_ref, hx_ref, pos_ref, x_hbm, stage, stage2, carry, sem, sem2, *, nt, ne, win, cap):
    i = pl.program_id(0)
    n_steps = pl.num_programs(0)
    b, j = i // nt, i % nt
    slot = i % 2
    d = hx_ref.shape[2]

    @pl.when(j == 0)
    def _():
        carry[...] = jnp.zeros(carry.shape, carry.dtype)

    hx = hx_ref[0]
    s0s, s1s, a0s = [], [], []
    for e in range(ne):
        s0, s1 = _tile_slots(offs_ref, b, j, e, nt, ne)
        s0s.append(s0)
        s1s.append(s1)
        a0s.append(_align_down(s0))

    def windows(c):
        q = jnp.concatenate([_one_hot_t(pos_ref[0, e:e + 1, :], a0s[e] + c * win, win) for e in range(ne)], axis=0)
        return jnp.dot(q, hx, preferred_element_type=F32)

    def chunk_copy(buf, e, c, semref):
        dst = x_hbm.at[e, b, pl.ds(pl.multiple_of(a0s[e] + c * win, BF16_ROWS), win), :]
        return pltpu.make_async_copy(buf.at[e], dst, semref.at[e])

    def first_chunk_copy(step, e):
        bb, jj = step // nt, step % nt
        a = _align_down(offs_ref[(bb * (nt + 1) + jj) * ne + e])
        sl = step % 2
        dst = x_hbm.at[e, bb, pl.ds(pl.multiple_of(a, BF16_ROWS), win), :]
        return pltpu.make_async_copy(stage.at[sl, e], dst, sem.at[sl, e])

    xw = windows(0).reshape(ne, win, d)
    head = xw[:, :BF16_ROWS, :] + carry[...].astype(F32)
    stage[slot, :, :BF16_ROWS, :] = head.astype(BF16)
    stage[slot, :, BF16_ROWS:, :] = xw[:, BF16_ROWS:, :].astype(BF16)

    @pl.when(i > 0)
    def _():
        for e in range(ne):
            first_chunk_copy(i - 1, e).wait()

    for e in range(ne):
        first_chunk_copy(i, e).start()
        off = _align_down(s1s[e]) - a0s[e]
        rows = stage[slot, e, pl.ds(pl.multiple_of(jnp.minimum(off, win - BF16_ROWS), BF16_ROWS), BF16_ROWS), :]
        carry[e] = jnp.where(off < win, rows, carry[e])

    nch = [lax.div(s1s[e] - a0s[e], win) + 1 for e in range(ne)]
    nmax = functools.reduce(jnp.maximum, nch)

    def extra(c, cr):
        xc = windows(c).reshape(ne, win, d)
        stage2[...] = xc.astype(BF16)
        for e in range(ne):
            @pl.when(c < nch[e])
            def _():
                cp = chunk_copy(stage2, e, c, sem2)
                cp.start()
                cp.wait()
                off = _align_down(s1s[e]) - a0s[e] - c * win

                @pl.when(jnp.logical_and(off >= 0, off < win))
                def _():
                    carry[e] = stage2[e, pl.ds(pl.multiple_of(off, BF16_ROWS), BF16_ROWS), :]
        return cr

    lax.fori_loop(1, nmax, extra, 0)

    if nt > 1:
        assert cap >= win

        @pl.when(j == 0)
        def _():
            stage2[0] = jnp.zeros((win, d), BF16)
            tails = [pltpu.make_async_copy(stage2.at[0], x_hbm.at[e, b, pl.ds(cap, win), :], sem2.at[e])
                     for e in range(ne)]
            for cp in tails:
                cp.start()
            for cp in tails:
                cp.wait()

    @pl.when(i == n_steps - 1)
    def _():
        for e in range(ne):
            first_chunk_copy(i, e).wait()


def dispatch_call(hx, pos, offs_flat, cap, tile, win):
    b, t, d = hx.shape
    ne = pos.shape[1]
    nt = t // tile
    cpad = cap + win if nt > 1 else win
    assert nt > 1 or cap < win
    gs = pltpu.PrefetchScalarGridSpec(
        num_scalar_prefetch=1,
        grid=(b * nt,),
        in_specs=[pl.BlockSpec((1, tile, d), lambda i, o: (i // nt, i % nt, 0)),
                  pl.BlockSpec((1, ne, tile), lambda i, o: (i // nt, 0, i % nt))],
        out_specs=pl.BlockSpec(memory_space=pl.ANY),
        scratch_shapes=[pltpu.VMEM((2, ne, win, d), BF16), pltpu.VMEM((ne, win, d), BF16),
                        pltpu.VMEM((ne, BF16_ROWS, d), BF16),
                        pltpu.SemaphoreType.DMA((2, ne)), pltpu.SemaphoreType.DMA((ne,))])
    return pl.pallas_call(
        functools.partial(_dispatch_body, nt=nt, ne=ne, win=win, cap=cap),
        grid_spec=gs,
        out_shape=jax.ShapeDtypeStruct((ne, b, cpad, d), BF16),
        compiler_params=_cparams(("arbitrary",)),
        name="dispatch",
    )(offs_flat, hx, pos)


def _ffn_body(x_ref, wg_ref, wu_ref, wd_ref, y_ref, wgb, wub, wdb, *, rows, rb, batched):
    def cast_weights():
        for src, dst in ((wg_ref, wgb), (wu_ref, wub), (wd_ref, wdb)):
            for r in range(0, src.shape[0], 256):
                dst[r:r + 256, :] = src[r:r + 256, :].astype(BF16)

    if batched:
        pl.when(pl.program_id(1) == 0)(cast_weights)
    else:
        cast_weights()
    for r in range(0, rows, rb):
        xb = x_ref[r:r + rb, :]
        a = jnp.dot(xb, wgb[...], preferred_element_type=F32)
        g = jnp.dot(xb, wub[...], preferred_element_type=F32)
        h = (jax.nn.silu(a) * g).astype(BF16)
        y_ref[r:r + rb, :] = jnp.dot(h, wdb[...], preferred_element_type=F32).astype(BF16)
    if y_ref.shape[0] > rows:
        y_ref[rows:, :] = jnp.zeros((y_ref.shape[0] - rows, y_ref.shape[1]), BF16)


def ffn_call(xs, wg, wu, wd, layer, cap):
    ne, b, cpad, d = xs.shape
    f = wg.shape[3]
    if cap % 256 == 0:
        grid = (ne, b)
        x_spec = pl.BlockSpec((None, None, cap, d), lambda e, i: (e, i, 0, 0))
        y_spec = pl.BlockSpec((None, None, cpad, d), lambda e, i: (e, i, 0, 0))
        wmap = lambda e, i: (layer, e, 0, 0)
        body = functools.partial(_ffn_body, rows=cap, rb=256, batched=True)
        sem = ("arbitrary", "arbitrary")
        args, shape = (xs, wg, wu, wd), xs.shape
    else:
        rows = b * cpad
        grid = (ne,)
        x_spec = pl.BlockSpec((None, rows, d), lambda e: (e, 0, 0))
        y_spec = pl.BlockSpec((None, rows, d), lambda e: (e, 0, 0))
        wmap = lambda e: (layer, e, 0, 0)
        body = functools.partial(_ffn_body, rows=rows, rb=rows, batched=False)
        sem = ("arbitrary",)
        args, shape = (xs.reshape(ne, rows, d), wg, wu, wd), (ne, rows, d)
    y = pl.pallas_call(
        body,
        grid=grid,
        in_specs=[x_spec, pl.BlockSpec((None, None, d, f), wmap), pl.BlockSpec((None, None, d, f), wmap),
                  pl.BlockSpec((None, None, f, d), wmap)],
        out_specs=y_spec,
        out_shape=jax.ShapeDtypeStruct(shape, BF16),
        scratch_shapes=[pltpu.VMEM((d, f), BF16), pltpu.VMEM((d, f), BF16), pltpu.VMEM((f, d), BF16)],
        compiler_params=_cparams(sem),
        name="ffn",
    )(*args)
    return y.reshape(xs.shape)


def _combine_body(offs_ref, x1_ref, pos_ref, aff_ref, mod_ref, g_ref, y_hbm, o_ref, ywin, ywin2, sem, sem2,
                  *, nt, ne, win):
    i = pl.program_id(0)
    n_steps = pl.num_programs(0)
    b, j = i // nt, i % nt
    slot = i % 2
    d = x1_ref.shape[2]
    cpad = y_hbm.shape[2]

    def window_copy(step, e, sl):
        bb, jj = step // nt, step % nt
        a = _align_down(offs_ref[(bb * (nt + 1) + jj) * ne + e])
        return pltpu.make_async_copy(y_hbm.at[e, bb, pl.ds(pl.multiple_of(a, BF16_ROWS), win), :],
                                     ywin.at[sl, e], sem.at[sl, e])

    @pl.when(i == 0)
    def _():
        for e in range(ne):
            window_copy(i, e, slot).start()

    @pl.when(i + 1 < n_steps)
    def _():
        for e in range(ne):
            window_copy(i + 1, e, 1 - slot).start()

    s1s, a0s = [], []
    for e in range(ne):
        s0, s1 = _tile_slots(offs_ref, b, j, e, nt, ne)
        s1s.append(s1)
        a0s.append(_align_down(s0))

    def weights(c):
        return jnp.concatenate(
            [_one_hot_t(pos_ref[0, e:e + 1, :], a0s[e] + c * win, win, aff_ref[0, e:e + 1, :]) for e in range(ne)],
            axis=0)

    pt = weights(0)
    for e in range(ne):
        window_copy(i, e, slot).wait()
    moe = lax.dot_general(pt, ywin[slot].reshape(ne * win, d), (((0,), (0,)), ((), ())), preferred_element_type=F32)

    nch = [lax.div(s1s[e] - a0s[e], win) + 1 for e in range(ne)]
    nmax = functools.reduce(jnp.maximum, nch)

    def extra(c, acc):
        for e in range(ne):
            a = jnp.minimum(a0s[e] + c * win, cpad - win)
            cp = pltpu.make_async_copy(y_hbm.at[e, b, pl.ds(pl.multiple_of(a, BF16_ROWS), win), :],
                                       ywin2.at[e], sem2.at[e])
            cp.start()
            cp.wait()
        return acc + lax.dot_general(weights(c), ywin2[...].reshape(ne * win, d), (((0,), (0,)), ((), ())),
                                     preferred_element_type=F32)

    moe = lax.fori_loop(1, nmax, extra, moe)
    m = mod_ref[0]
    o_ref[0] = x1_ref[0] + m[5:6] * _rms(moe, g_ref[...])


def combine_call(x1, ys, pos, afft, offs_flat, mod3, mod_row, g_post, tile, win):
    b, t, d = x1.shape
    ne = pos.shape[1]
    nt = t // tile
    mod_idx = (lambda i, o: (i // nt, 0, 0)) if mod_row is None else (lambda i, o: (mod_row, 0, 0))
    gs = pltpu.PrefetchScalarGridSpec(
        num_scalar_prefetch=1,
        grid=(b * nt,),
        in_specs=[pl.BlockSpec((1, tile, d), lambda i, o: (i // nt, i % nt, 0)),
                  pl.BlockSpec((1, ne, tile), lambda i, o: (i // nt, 0, i % nt)),
                  pl.BlockSpec((1, ne, tile), lambda i, o: (i // nt, 0, i % nt)),
                  pl.BlockSpec((1, 6, d), mod_idx),
                  pl.BlockSpec((1, d), lambda i, o: (0, 0)),
                  pl.BlockSpec(memory_space=pl.ANY)],
        out_specs=pl.BlockSpec((1, tile, d), lambda i, o: (i // nt, i % nt, 0)),
        scratch_shapes=[pltpu.VMEM((2, ne, win, d), BF16), pltpu.VMEM((ne, win, d), BF16),
                        pltpu.SemaphoreType.DMA((2, ne)), pltpu.SemaphoreType.DMA((ne,))])
    return pl.pallas_call(
        functools.partial(_combine_body, nt=nt, ne=ne, win=win),
        grid_spec=gs,
        out_shape=jax.ShapeDtypeStruct((b, t, d), F32),
        compiler_params=_cparams(("arbitrary",)),
        name="combine",
    )(offs_flat, x1, pos, afft, mod3, g_post.reshape(1, d), ys)


def ec_moe_call(x1, hx, afft, mod3, mod_row, g_post, wg, wu, wd, layer):
    b, t, d = x1.shape
    ne = afft.shape[1]
    cap = EC_CAPACITY_FACTOR * t // ne
    tile = min(256, t)
    win = min(cap, 3 * EC_CAPACITY_FACTOR * tile // (2 * ne)) + BF16_ROWS
    win = -(-win // BF16_ROWS) * BF16_ROWS
    pos, offs_flat = select_call(afft, cap, tile)
    xs = dispatch_call(hx, pos, offs_flat, cap, tile, win)
    ys = ffn_call(xs, wg, wu, wd, layer, cap)
    return combine_call(x1, ys, pos, afft, offs_flat, mod3, mod_row, g_post, tile, win)


def kernel(x, c, ctx, c_ctx, w_mod, b_mod, g_mix_pre, g_mix_post, g_ffn_pre, g_ffn_post, w_in, s5_lam_re, s5_lam_im,
           s5_log_dt, s5_b_re, s5_b_im, s5_c_re, s5_c_im, s5_d, s5_w_glu, pool_w, pool_scale, w_out, w_router,
           w_gate, w_up, w_down):
    depth = w_mod.shape[0]
    b, t, d = x.shape
    _, _, g, p, h = s5_b_re.shape
    ds = g * h
    dp = d - ds
    assert t % min(512, t) == 0 and t % LANES == 0 and t % GRID_W == 0
    assert ctx.shape[1] % LANES == 0 and 2 * p == LANES
    nmod = w_mod.shape[2] // d
    rows = -(-(b + 1) // SUBLANES) * SUBLANES
    cc = jnp.zeros((rows, d), F32).at[:b].set(c.astype(F32)).at[b].set(c_ctx.astype(F32))
    ncl, ncc = t // LANES, ctx.shape[1] // LANES

    for l in range(depth):
        last = l == depth - 1
        mod3 = mod_call(cc, w_mod[l].astype(F32), b_mod[l].astype(F32)).reshape(rows, nmod, d)
        ws_t = jnp.transpose(w_in[l][:, :ds]).astype(BF16)
        wp = w_in[l][:, ds:].astype(BF16)
        wtoe, ptab, htab, dr, di = s5_tables(s5_lam_re[l], s5_lam_im[l], s5_log_dt[l], s5_b_re[l], s5_b_im[l],
                                             s5_c_re[l], s5_c_im[l], s5_d[l])
        pw = pool_w[l].astype(BF16)
        ps = pool_scale[l].astype(F32).reshape(1, dp)
        wglu_t = jnp.transpose(s5_w_glu[l]).astype(BF16)
        wo_s, wo_p = w_out[l][:ds].astype(BF16), w_out[l][ds:].astype(BF16)
        wr2 = router_split(w_router[l])
        ne = w_router.shape[2]
        wg, wu, wd = w_gate.astype(F32), w_up.astype(F32), w_down.astype(F32)

        pxt, pxp = mixer_in_call(x, mod3, None, g_mix_pre[l], ws_t, wp)
        pct, pcp = mixer_in_call(ctx, mod3, b, g_mix_pre[l], ws_t, wp)
        xl, xc = pxt.reshape(ncl * b, g, h, LANES), pct.reshape(ncc * b, g, h, LANES)
        sl, sc = s5_states_call(xl, xc, ptab, htab)
        hl, hc = s5_recur_call(sl.reshape(N_HALF, ncl, b, -1), sc.reshape(N_HALF, ncc, b, -1), dr, di, p)
        hl, hc = hl.reshape(N_HALF, ncl * b, -1), hc.reshape(N_HALF, ncc * b, -1)
        if last:
            (yt,) = s5_main_call([xl], [hl], wtoe, ptab, htab)
        else:
            yt, yc = s5_main_call([xl, xc], [hl, hc], wtoe, ptab, htab)
        pool_x = pool_grid_call(pxp, pw, ps)
        x1, hx, afft = mixer_out_call(x, yt.reshape(ncl, b, ds, LANES), pool_x, mod3, None, wglu_t, wo_s, wo_p,
                                      g_mix_post[l], g_ffn_pre[l], wr2, ne)
        x = ec_moe_call(x1, hx, afft, mod3, None, g_ffn_post[l], wg, wu, wd, l)
        if not last:
            pool_c = pool_seq_call(pcp, pw, ps)
            c1, hcx, affc = mixer_out_call(ctx, yc.reshape(ncc, b, ds, LANES), pool_c, mod3, b, wglu_t, wo_s, wo_p,
                                           g_mix_post[l], g_ffn_pre[l], wr2, ne)
            ctx = ec_moe_call(c1, hcx, affc, mod3, b, g_ffn_post[l], wg, wu, wd, l)
    return x
```

```python
import functools
import math

import numpy as np
import jax
import jax.numpy as jnp
from jax import lax
from jax.experimental import pallas as pl
from jax.experimental.pallas import tpu as pltpu

GRID_W = 64
POOL_WINDOWS = (2, 4, 8, 16)
EC_CAPACITY_FACTOR = 2
RMS_EPS = 1e-6

LANES = 128
CHUNK = LANES // 2
SUBLANES = 8
BF16_ROWS = 16
VMEM_LIMIT = 56 * 1024 * 1024

F32 = jnp.float32
BF16 = jnp.bfloat16
HI = lax.Precision.HIGHEST


def _cparams(sem):
    return pltpu.CompilerParams(dimension_semantics=sem, vmem_limit_bytes=VMEM_LIMIT)


def _rms(v, g):
    return v * lax.rsqrt(jnp.mean(v * v, axis=-1, keepdims=True) + RMS_EPS) * g


def _mod_body(c_ref, w_ref, b_ref, o_ref):
    s = jax.nn.silu(c_ref[...])
    o_ref[...] = jnp.dot(s, w_ref[...], preferred_element_type=F32, precision=HI) + b_ref[...]


def mod_call(cc, wm, bm):
    r, d = cc.shape
    n = wm.shape[1]
    return pl.pallas_call(
        _mod_body,
        grid=(n // d,),
        in_specs=[pl.BlockSpec((r, d), lambda j: (0, 0)),
                  pl.BlockSpec((d, d), lambda j: (0, j)),
                  pl.BlockSpec((1, d), lambda j: (0, j))],
        out_specs=pl.BlockSpec((r, d), lambda j: (0, j)),
        out_shape=jax.ShapeDtypeStruct((r, n), F32),
        compiler_params=_cparams(("arbitrary",)),
        name="mod",
    )(cc, wm, bm.reshape(1, n))


def _mixer_in_body(x_ref, mod_ref, g_ref, ws_ref, wp_ref, pxt_ref, pxp_ref):
    m = mod_ref[0]
    h = _rms(x_ref[0], g_ref[...] * (1.0 + m[1:2])) + m[0:1]
    hb = h.astype(BF16)
    pxp_ref[0] = jnp.dot(hb, wp_ref[...], preferred_element_type=F32).astype(BF16)
    pt = lax.dot_general(ws_ref[...], hb, (((1,), (1,)), ((), ())), preferred_element_type=F32)
    for j in range(pxt_ref.shape[0]):
        pxt_ref[j] = pt[:, LANES * j:LANES * (j + 1)]


def mixer_in_call(x, mod3, mod_row, g, ws_t, wp):
    b, t, d = x.shape
    ds, dp = ws_t.shape[0], wp.shape[1]
    tm = min(512, t)
    nj = tm // LANES
    mod_idx = (lambda i, j: (i, 0, 0)) if mod_row is None else (lambda i, j: (mod_row, 0, 0))
    return pl.pallas_call(
        _mixer_in_body,
        grid=(b, t // tm),
        in_specs=[pl.BlockSpec((1, tm, d), lambda i, j: (i, j, 0)),
                  pl.BlockSpec((1, 6, d), mod_idx),
                  pl.BlockSpec((1, d), lambda i, j: (0, 0)),
                  pl.BlockSpec((ds, d), lambda i, j: (0, 0)),
                  pl.BlockSpec((d, dp), lambda i, j: (0, 0))],
        out_specs=[pl.BlockSpec((nj, None, ds, LANES), lambda i, j: (j, i, 0, 0)),
                   pl.BlockSpec((1, tm, dp), lambda i, j: (i, j, 0))],
        out_shape=[jax.ShapeDtypeStruct((t // LANES, b, ds, LANES), F32),
                   jax.ShapeDtypeStruct((b, t, dp), BF16)],
        compiler_params=_cparams(("arbitrary", "arbitrary")),
        name="mixer_in",
    )(x, mod3, g.reshape(1, d), ws_t, wp)


def s5_tables(lam_re, lam_im, log_dt, b_re, b_im, c_re, c_im, d_skip):
    n_dir, g, p = lam_re.shape
    h = b_re.shape[-1]
    lc = CHUNK
    lam_re, lam_im = lam_re.astype(F32), lam_im.astype(F32)
    dt = jnp.exp(log_dt.astype(F32))[..., None]
    zr, zi = lam_re * dt, lam_im * dt
    er = jnp.exp(zr)
    ar, ai = er * jnp.cos(zi), er * jnp.sin(zi)
    den = lam_re * lam_re + lam_im * lam_im
    qr = ((ar - 1.0) * lam_re + ai * lam_im) / den
    qi = (ai * lam_re - (ar - 1.0) * lam_im) / den
    br, bi = b_re.astype(F32), b_im.astype(F32)
    bbr = qr[..., None] * br - qi[..., None] * bi
    bbi = qr[..., None] * bi + qi[..., None] * br
    cr, ci = c_re.astype(F32), c_im.astype(F32)
    jj = jnp.arange(lc + 1, dtype=F32)
    pe = jnp.exp(zr[..., None] * jj)
    pr, pi = pe * jnp.cos(zi[..., None] * jj), pe * jnp.sin(zi[..., None] * jj)
    e_r = jnp.einsum('dgop,dgpi->dgpoi', cr, bbr, precision=HI) - jnp.einsum('dgop,dgpi->dgpoi', ci, bbi, precision=HI)
    e_i = jnp.einsum('dgop,dgpi->dgpoi', cr, bbi, precision=HI) + jnp.einsum('dgop,dgpi->dgpoi', ci, bbr, precision=HI)
    kk = (jnp.einsum('dgpj,dgpoi->dgjoi', pr[..., :lc], e_r, precision=HI)
          - jnp.einsum('dgpj,dgpoi->dgjoi', pi[..., :lc], e_i, precision=HI))
    kf = jnp.transpose(kk[0], (0, 3, 2, 1))
    kr = jnp.transpose(kk[1], (0, 3, 2, 1))
    dsk = d_skip.astype(F32).reshape(g, h)
    eye = jnp.eye(h, dtype=F32)[None] * dsk[:, :, None]
    w0 = kf[..., 0] + kr[..., 0] + eye
    wtoe = jnp.concatenate([w0[..., None], kf[..., 1:], jnp.zeros_like(w0)[..., None], kr[..., :0:-1]], axis=-1)
    odd_ho = (jnp.arange(h) % 2 == 1)[None, None, :, None]
    wtoe = jnp.where(odd_ho, jnp.roll(wtoe, lc, axis=-1), wtoe).reshape(g, h * h, 2 * lc)
    tp = lambda a: jnp.swapaxes(a, -1, -2)
    lanes = lambda parts: jnp.concatenate(parts, axis=-1)
    prf, pif, prr, pir = tp(pr[0]), tp(pi[0]), tp(pr[1]), tp(pi[1])
    q_f_r, q_f_i = prf[:, lc - 1::-1][:, :lc], pif[:, lc - 1::-1][:, :lc]
    q_r_r, q_r_i = prr[:, :lc], pir[:, :lc]
    i_f_r, i_f_i = prf[:, 1:lc + 1], pif[:, 1:lc + 1]
    i_r_r, i_r_i = prr[:, lc:0:-1], pir[:, lc:0:-1]
    pw = jnp.stack([lanes([q_f_r, q_f_r, q_r_r, q_r_r]), lanes([q_f_i, q_f_i, q_r_i, q_r_i]),
                    lanes([i_f_r, i_f_i, i_r_r, i_r_i]), lanes([i_f_i, i_f_r, i_r_i, i_r_r])], axis=1)
    b_r, b_i = tp(bbr), tp(bbi)
    hb = jnp.stack([lanes([b_r[0], b_i[0], b_r[1], b_i[1]]), lanes([-b_i[0], b_r[0], -b_i[1], b_r[1]]),
                    lanes([cr[0], -cr[0], cr[1], -cr[1]]), lanes([-ci[0], -ci[0], -ci[1], -ci[1]])], axis=1)
    dre = jnp.stack([pr[0][..., lc], pr[0][..., lc], pr[1][..., lc], pr[1][..., lc]], axis=1)
    dim_ = jnp.stack([-pi[0][..., lc], pi[0][..., lc], -pi[1][..., lc], pi[1][..., lc]], axis=1)
    return wtoe, pw, hb, dre.reshape(1, g * 4 * p), dim_.reshape(1, g * 4 * p)


def _low_half(shape):
    return lax.broadcasted_iota(jnp.int32, shape, len(shape) - 1) < CHUNK


def _gather_chunk_rows(x_ref, vbuf, r0, rows):
    nh = x_ref.shape[1]
    lo = _low_half((SUBLANES, LANES))
    for c0 in range(0, rows, SUBLANES):
        for m in range(nh // 2):
            a = x_ref[pl.ds(r0 + c0, SUBLANES), 2 * m, :]
            b = x_ref[pl.ds(r0 + c0, SUBLANES), 2 * m + 1, :]
            vbuf[0, pl.ds(r0 + c0, SUBLANES), LANES * m:LANES * (m + 1)] = jnp.where(lo, a, pltpu.roll(b, CHUNK, 1))
            vbuf[1, pl.ds(r0 + c0, SUBLANES), LANES * m:LANES * (m + 1)] = jnp.where(lo, pltpu.roll(a, CHUNK, 1), b)


def _scatter_chunk_rows(vbuf, o_ref, r0, rows):
    nh = o_ref.shape[1]
    lo = _low_half((SUBLANES, LANES))
    for c0 in range(0, rows, SUBLANES):
        for m in range(nh // 2):
            p0 = vbuf[0, pl.ds(r0 + c0, SUBLANES), LANES * m:LANES * (m + 1)]
            p1 = vbuf[1, pl.ds(r0 + c0, SUBLANES), LANES * m:LANES * (m + 1)]
            o_ref[pl.ds(r0 + c0, SUBLANES), 2 * m, :] = jnp.where(lo, p0, pltpu.roll(p1, CHUNK, 1))
            o_ref[pl.ds(r0 + c0, SUBLANES), 2 * m + 1, :] = jnp.where(lo, pltpu.roll(p0, CHUNK, 1), p1)


def _row_block(nc):
    for rb in (256, 128, 64, 32, 16, 8):
        if nc % rb == 0:
            return rb
    raise ValueError(f"chunk-row count {nc} must be a multiple of {SUBLANES}")


def _build_state_table(pw_ref, hb_ref, k, tab):
    nh = hb_ref.shape[1]
    pa, pb = pw_ref[k], pw_ref[k + 1]
    for h in range(nh):
        tab[CHUNK * h:CHUNK * (h + 1), :] = (hb_ref[k, h:h + 1, :] * pa + hb_ref[k + 1, h:h + 1, :] * pb).astype(BF16)


def _for_row_blocks(nc, rb_max, fn):
    rb = min(rb_max, nc)
    assert nc % rb == 0
    if nc // rb <= 4:
        for i in range(nc // rb):
            fn(i * rb, rb)
        return

    def blk(i, carry):
        fn(pl.multiple_of(i * rb, rb), rb)
        return carry

    lax.fori_loop(0, nc // rb, blk, 0)


N_HALF = LANES // CHUNK


def _s5_states_body(xl_ref, xc_ref, pw_ref, hb_ref, ol_ref, oc_ref, wb, vbuf, *, rb_max):
    _build_state_table(pw_ref, hb_ref, 0, wb)
    for x_ref, o_ref in ((xl_ref, ol_ref), (xc_ref, oc_ref)):
        def blk(r0, rb, x_ref=x_ref, o_ref=o_ref):
            _gather_chunk_rows(x_ref, vbuf, r0, rb)
            for q in range(N_HALF):
                o_ref[q, pl.ds(r0, rb), :] = jnp.dot(vbuf[q, pl.ds(r0, rb), :].astype(BF16), wb[...],
                                                     preferred_element_type=F32)

        _for_row_blocks(x_ref.shape[0], rb_max, blk)


def s5_states_call(pxt, pct, pw, hb):
    ncl, g, h, _ = pxt.shape
    ncc = pct.shape[0]
    ns = pw.shape[-1]
    rb = _row_block(ncl)
    tab = lambda i: (i, 0, 0, 0)
    assert ncc <= ncl
    return pl.pallas_call(
        functools.partial(_s5_states_body, rb_max=rb),
        grid=(g,),
        in_specs=[pl.BlockSpec((ncl, None, h, LANES), lambda i: (0, i, 0, 0)),
                  pl.BlockSpec((ncc, None, h, LANES), lambda i: (0, i, 0, 0)),
                  pl.BlockSpec((None, 4, CHUNK, ns), tab),
                  pl.BlockSpec((None, 4, h, ns), tab)],
        out_specs=[pl.BlockSpec((N_HALF, ncl, ns), lambda i: (0, 0, i)),
                   pl.BlockSpec((N_HALF, ncc, ns), lambda i: (0, 0, i))],
        out_shape=[jax.ShapeDtypeStruct((N_HALF, ncl, g * ns), F32), jax.ShapeDtypeStruct((N_HALF, ncc, g * ns), F32)],
        scratch_shapes=[pltpu.VMEM((h * CHUNK, ns), BF16), pltpu.VMEM((N_HALF, ncl, h * CHUNK), F32)],
        compiler_params=_cparams(("arbitrary",)),
        name="s5_states",
    )(pxt, pct, pw, hb)


def _s5_recur_body(sl_ref, sc_ref, dr_ref, di_ref, hl_ref, hc_ref, *, half):
    ncl, ncc = sl_ref.shape[1], sc_ref.shape[1]
    shp = sl_ref.shape[2:]
    lane = lax.broadcasted_iota(jnp.int32, shp, 1)
    is_re = (lane % (2 * half)) < half
    is_fwd = (lane % (4 * half)) < 2 * half
    dr = jnp.broadcast_to(dr_ref[...], shp)
    di = jnp.broadcast_to(di_ref[...], shp)
    nl = shp[1]

    def swap(v):
        return jnp.where(is_re, pltpu.roll(v, nl - half, 1), pltpu.roll(v, half, 1))

    def step(hs, s):
        h, hsw = hs
        return dr * h + di * hsw + s, dr * hsw - di * h + swap(s)

    zero = (jnp.zeros(shp, F32), jnp.zeros(shp, F32))

    def fwd(s_ref, h_ref):
        def body(k, hs):
            for q in range(N_HALF):
                h_ref[q, k] = hs[0]
                hs = step(hs, s_ref[q, k])
            return hs
        return body

    hs = lax.fori_loop(0, ncc, fwd(sc_ref, hc_ref), zero)
    lax.fori_loop(0, ncl, fwd(sl_ref, hl_ref), hs)

    def rev(s_ref, h_ref, n):
        def body(i, hs):
            k = n - 1 - i
            for q in reversed(range(N_HALF)):
                h_ref[q, k] = jnp.where(is_fwd, h_ref[q, k], hs[0])
                hs = step(hs, s_ref[q, k])
            return hs
        return body

    hs = lax.fori_loop(0, ncc, rev(sc_ref, hc_ref, ncc), zero)
    lax.fori_loop(0, ncl, rev(sl_ref, hl_ref, ncl), hs)


def s5_recur_call(sl, sc, dr, di, half):
    _, ncl, b, s = sl.shape
    ncc = sc.shape[1]
    lb = 1024 if s % 1024 == 0 else s
    blk = lambda n: pl.BlockSpec((N_HALF, n, b, lb), lambda i: (0, 0, 0, i))
    return pl.pallas_call(
        functools.partial(_s5_recur_body, half=half),
        grid=(s // lb,),
        in_specs=[blk(ncl), blk(ncc),
                  pl.BlockSpec((1, lb), lambda i: (0, i)),
                  pl.BlockSpec((1, lb), lambda i: (0, i))],
        out_specs=[blk(ncl), blk(ncc)],
        out_shape=[jax.ShapeDtypeStruct(sl.shape, F32), jax.ShapeDtypeStruct(sc.shape, F32)],
        compiler_params=_cparams(("arbitrary",)),
        name="s5_recur",
    )(sl, sc, dr, di)


def _s5_main_body(*refs, n_sets, rb_max):
    x_refs, hin_refs = refs[:n_sets], refs[n_sets:2 * n_sets]
    wt_ref, pw_ref, hb_ref = refs[2 * n_sets:2 * n_sets + 3]
    o_refs = refs[2 * n_sets + 3:3 * n_sets + 3]
    mt, wct, vbuf = refs[3 * n_sets + 3:]
    nh = x_refs[0].shape[1]
    lo = _low_half((CHUNK, LANES))

    for hi in range(nh):
        for m in range(nh // 2):
            we = wt_ref[hi * nh + 2 * m:hi * nh + 2 * m + 1, :]
            wo = wt_ref[hi * nh + 2 * m + 1:hi * nh + 2 * m + 2, :]
            te = pltpu.roll(jnp.broadcast_to(we, (CHUNK, LANES)), 0, 1, stride=1, stride_axis=0)
            to = pltpu.roll(jnp.broadcast_to(wo, (CHUNK, LANES)), 0, 1, stride=1, stride_axis=0)
            mt[hi * CHUNK:(hi + 1) * CHUNK, LANES * m:LANES * (m + 1)] = jnp.where(lo, te, to).astype(BF16)
    _build_state_table(pw_ref, hb_ref, 2, wct)

    for x_ref, hin_ref, o_ref in zip(x_refs, hin_refs, o_refs):
        def blk(r0, rb, x_ref=x_ref, hin_ref=hin_ref, o_ref=o_ref):
            _gather_chunk_rows(x_ref, vbuf, r0, rb)
            for q in range(N_HALF):
                y = jnp.dot(vbuf[q, pl.ds(r0, rb), :].astype(BF16), mt[...], preferred_element_type=F32)
                y = y + lax.dot_general(hin_ref[q, pl.ds(r0, rb), :].astype(BF16), wct[...],
                                        (((1,), (1,)), ((), ())), preferred_element_type=F32)
                vbuf[q, pl.ds(r0, rb), :] = y
            _scatter_chunk_rows(vbuf, o_ref, r0, rb)

        _for_row_blocks(x_ref.shape[0], rb_max, blk)


def s5_main_call(xs, hins, wtoe, pw, hb):
    n_sets = len(xs)
    _, g, h, _ = xs[0].shape
    ns = pw.shape[-1]
    rb = _row_block(xs[0].shape[0])
    tab = lambda i: (i, 0, 0, 0)
    x_spec = lambda x: pl.BlockSpec((x.shape[0], None, h, LANES), lambda i: (0, i, 0, 0))
    return pl.pallas_call(
        functools.partial(_s5_main_body, n_sets=n_sets, rb_max=rb),
        grid=(g,),
        in_specs=[x_spec(x) for x in xs]
        + [pl.BlockSpec((N_HALF, x.shape[0], ns), lambda i: (0, 0, i)) for x in xs]
        + [pl.BlockSpec((None, h * h, 2 * CHUNK), lambda i: (i, 0, 0)),
           pl.BlockSpec((None, 4, CHUNK, ns), tab),
           pl.BlockSpec((None, 4, h, ns), tab)],
        out_specs=[x_spec(x) for x in xs],
        out_shape=[jax.ShapeDtypeStruct(x.shape, F32) for x in xs],
        scratch_shapes=[pltpu.VMEM((h * CHUNK, h * CHUNK), BF16), pltpu.VMEM((h * CHUNK, ns), BF16),
                        pltpu.VMEM((N_HALF, max(x.shape[0] for x in xs), h * CHUNK), F32)],
        compiler_params=_cparams(("arbitrary",)),
        name="s5_main",
    )(*xs, *hins, wtoe, pw, hb)


def _band(n, w):
    i = np.arange(n)
    return ((i[None, :] >= i[:, None] - w // 2) & (i[None, :] < i[:, None] + w // 2)).astype(np.float32)


def _cnt(n, w):
    i = np.arange(n)
    return (np.clip(i + w // 2, 0, n) - np.clip(i - w // 2, 0, n)).astype(np.float32)


def _pool_grid_body(wv_ref, x_ref, cb_ref, rc_ref, pw_ref, ps_ref, o_ref, cs, *, tb, halo):
    t = x_ref.shape[1]
    wh = wv_ref[pl.program_id(0)] // 2
    zeros = jnp.zeros((halo, LANES), F32)
    cs[0:halo, :] = zeros
    cs[halo + t:halo + t + halo, :] = zeros
    for i in range(t // tb):
        cs[halo + i * tb:halo + (i + 1) * tb, :] = jnp.dot(cb_ref[...], x_ref[0, i * tb:(i + 1) * tb, :],
                                                            preferred_element_type=F32)
    def grid_row(r):
        return cs[pl.ds(pl.multiple_of(halo + r * GRID_W, GRID_W), GRID_W), :]

    run = lax.fori_loop(0, wh - 1, lambda r, acc: acc + grid_row(r), jnp.zeros((GRID_W, LANES), F32))
    rpb = tb // GRID_W

    def blk(i, run):
        rows = []
        for q in range(rpb):
            r = i * rpb + q
            run = run + grid_row(r + wh - 1) - grid_row(r - wh - 1)
            rows.append(run)
        tok = pl.multiple_of(i * tb, tb)
        v = x_ref[0, pl.ds(tok, tb), :].astype(F32)
        z = jnp.concatenate(rows, axis=0) * rc_ref[pl.ds(tok, tb), :] - v
        o = jnp.dot(z.astype(BF16), pw_ref[...], preferred_element_type=F32) * ps_ref[...]
        o_ref[0, pl.ds(tok, tb), :] = o.astype(BF16)
        return run

    lax.fori_loop(0, t // tb, blk, run)


def pool_grid_call(pxp, pw, ps):
    b, t, dp = pxp.shape
    ng = len(POOL_WINDOWS)
    cg = dp // ng
    rows = t // GRID_W
    tb = min(512, t)
    halo = (max(POOL_WINDOWS) // 2 + 1) * GRID_W
    cb = np.stack([np.kron(np.eye(tb // GRID_W, dtype=np.float32), _band(GRID_W, w)) for w in POOL_WINDOWS])
    rc = np.stack([1.0 / (np.repeat(_cnt(rows, w), GRID_W) * np.tile(_cnt(GRID_W, w), rows)) for w in POOL_WINDOWS])
    rc = np.broadcast_to(rc[:, :, None], (ng, t, cg)).astype(np.float32)
    gs = pltpu.PrefetchScalarGridSpec(
        num_scalar_prefetch=1,
        grid=(ng, b),
        in_specs=[pl.BlockSpec((1, t, cg), lambda g, i, wv: (i, 0, g)),
                  pl.BlockSpec((None, tb, tb), lambda g, i, wv: (g, 0, 0)),
                  pl.BlockSpec((None, t, cg), lambda g, i, wv: (g, 0, 0)),
                  pl.BlockSpec((None, cg, cg), lambda g, i, wv: (g, 0, 0)),
                  pl.BlockSpec((1, cg), lambda g, i, wv: (0, g))],
        out_specs=pl.BlockSpec((1, t, cg), lambda g, i, wv: (i, 0, g)),
        scratch_shapes=[pltpu.VMEM((t + 2 * halo, cg), F32)])
    return pl.pallas_call(
        functools.partial(_pool_grid_body, tb=tb, halo=halo),
        grid_spec=gs,
        out_shape=jax.ShapeDtypeStruct((b, t, dp), BF16),
        compiler_params=_cparams(("arbitrary", "arbitrary")),
        name="pool_grid",
    )(jnp.asarray(POOL_WINDOWS, jnp.int32), pxp, jnp.asarray(cb, BF16), jnp.asarray(rc), pw, ps)


def _pool_seq_body(x_ref, cb_ref, rc_ref, pw_ref, ps_ref, o_ref):
    v = x_ref[0]
    m = jnp.dot(cb_ref[...], v, preferred_element_type=F32) * rc_ref[...]
    z = m - v.astype(F32)
    o_ref[0] = (jnp.dot(z.astype(BF16), pw_ref[...], preferred_element_type=F32) * ps_ref[...]).astype(BF16)


def pool_seq_call(pcp, pw, ps):
    b, t, dp = pcp.shape
    ng = len(POOL_WINDOWS)
    cg = dp // ng
    cb = np.stack([_band(t, w) for w in POOL_WINDOWS])
    rc = np.stack([1.0 / _cnt(t, w) for w in POOL_WINDOWS])
    rc = np.broadcast_to(rc[:, :, None], (ng, t, cg)).astype(np.float32)
    return pl.pallas_call(
        _pool_seq_body,
        grid=(ng, b),
        in_specs=[pl.BlockSpec((1, t, cg), lambda g, i: (i, 0, g)),
                  pl.BlockSpec((None, t, t), lambda g, i: (g, 0, 0)),
                  pl.BlockSpec((None, t, cg), lambda g, i: (g, 0, 0)),
                  pl.BlockSpec((None, cg, cg), lambda g, i: (g, 0, 0)),
                  pl.BlockSpec((1, cg), lambda g, i: (0, g))],
        out_specs=pl.BlockSpec((1, t, cg), lambda g, i: (i, 0, g)),
        out_shape=jax.ShapeDtypeStruct((b, t, dp), BF16),
        compiler_params=_cparams(("arbitrary", "arbitrary")),
        name="pool_seq",
    )(pcp, jnp.asarray(cb, BF16), jnp.asarray(rc), pw, ps)


def _mixer_out_body(x_ref, yt_ref, pool_ref, mod_ref, wg_ref, wos_ref, wop_ref, gpost_ref, gpre_ref, wr_ref,
                    x1_ref, hx_ref, aff_ref):
    m = mod_ref[0]
    yt = jnp.concatenate([yt_ref[j] for j in range(yt_ref.shape[0])], axis=1)
    ge = jax.nn.gelu(yt)
    zt = jnp.dot(wg_ref[...], ge.astype(BF16), preferred_element_type=F32)
    s5o = (ge * jax.nn.sigmoid(zt)).astype(BF16)
    mix = lax.dot_general(s5o, wos_ref[...], (((0,), (0,)), ((), ())), preferred_element_type=F32)
    mix = mix + jnp.dot(pool_ref[0], wop_ref[...], preferred_element_type=F32)
    x1 = x_ref[0] + _rms(mix, gpost_ref[...] * m[2:3])
    x1_ref[0] = x1
    h2 = _rms(x1, gpre_ref[...] * (1.0 + m[4:5])) + m[3:4]
    h_hi = h2.astype(BF16)
    h_lo = (h2 - h_hi.astype(F32)).astype(BF16)
    hx_ref[0] = h_hi
    ne = aff_ref.shape[1]
    nt_dims = (((1,), (1,)), ((), ()))
    t1 = lax.dot_general(wr_ref[...], h_hi, nt_dims, preferred_element_type=F32)
    t2 = lax.dot_general(wr_ref[...], h_lo, nt_dims, preferred_element_type=F32)
    lg = t1[0:ne] + t1[ne:2 * ne] + t2[0:ne]
    e = jnp.exp(lg - jnp.max(lg, axis=0, keepdims=True))
    aff_ref[0] = e / jnp.sum(e, axis=0, keepdims=True)


def router_split(w_router):
    d, ne = w_router.shape
    rows = -(-2 * ne // BF16_ROWS) * BF16_ROWS
    w = jnp.transpose(w_router.astype(F32))
    hi = w.astype(BF16)
    lo = (w - hi.astype(F32)).astype(BF16)
    return jnp.concatenate([hi, lo, jnp.zeros((rows - 2 * ne, d), BF16)], axis=0)


def mixer_out_call(x, yt, pool, mod3, mod_row, wglu_t, wo_s, wo_p, g_post, g_pre, wr2, ne):
    b, t, d = x.shape
    ds, dp = wo_s.shape[0], wo_p.shape[0]
    tm = min(512, t)
    nj = tm // LANES
    mod_idx = (lambda i, j: (i, 0, 0)) if mod_row is None else (lambda i, j: (mod_row, 0, 0))
    const = lambda i, j: (0, 0)
    return pl.pallas_call(
        _mixer_out_body,
        grid=(b, t // tm),
        in_specs=[pl.BlockSpec((1, tm, d), lambda i, j: (i, j, 0)),
                  pl.BlockSpec((nj, None, ds, LANES), lambda i, j: (j, i, 0, 0)),
                  pl.BlockSpec((1, tm, dp), lambda i, j: (i, j, 0)),
                  pl.BlockSpec((1, 6, d), mod_idx),
                  pl.BlockSpec((ds, ds), const),
                  pl.BlockSpec((ds, d), const),
                  pl.BlockSpec((dp, d), const),
                  pl.BlockSpec((1, d), const),
                  pl.BlockSpec((1, d), const),
                  pl.BlockSpec(wr2.shape, const)],
        out_specs=[pl.BlockSpec((1, tm, d), lambda i, j: (i, j, 0)),
                   pl.BlockSpec((1, tm, d), lambda i, j: (i, j, 0)),
                   pl.BlockSpec((1, ne, tm), lambda i, j: (i, 0, j))],
        out_shape=[jax.ShapeDtypeStruct((b, t, d), F32),
                   jax.ShapeDtypeStruct((b, t, d), BF16),
                   jax.ShapeDtypeStruct((b, ne, t), F32)],
        compiler_params=_cparams(("arbitrary", "arbitrary")),
        name="mixer_out",
    )(x, yt, pool, mod3, wglu_t, wo_s, wo_p, g_post.reshape(1, d), g_pre.reshape(1, d), wr2)


def _select_body(aff_ref, tri_ref, pos_ref, offs_ref, *, cap, tile):
    a = aff_ref[0]
    ne, t = a.shape
    bits = pltpu.bitcast(a, jnp.int32)

    def count_ge(thr):
        return jnp.sum((bits >= thr).astype(jnp.int32), axis=1, keepdims=True)

    def bis(_, c):
        lo, hi = c
        mid = lo + lax.shift_right_logical(hi - lo, 1)
        ok = count_ge(mid) >= cap
        return jnp.where(ok, mid, lo), jnp.where(ok, hi, mid)

    lo0 = jnp.zeros((ne, 1), jnp.int32)
    hi0 = jnp.full((ne, 1), 0x7F800001, jnp.int32)
    thr, _ = lax.fori_loop(0, 32, bis, (lo0, hi0))
    gt = bits > thr
    eq = bits == thr
    need = (cap - jnp.sum(gt.astype(jnp.int32), axis=1, keepdims=True)).astype(F32)
    tri = tri_ref[...]
    nblk = t // tile

    def cumsum(mask_f32):
        run = jnp.zeros((ne, 1), F32)
        pieces, starts = [], []
        for j in range(nblk):
            starts.append(run)
            cj = jnp.dot(mask_f32[:, tile * j:tile * (j + 1)].astype(BF16), tri, preferred_element_type=F32) + run
            pieces.append(cj)
            run = cj[:, tile - 1:tile]
        starts.append(run)
        return jnp.concatenate(pieces, axis=1) if nblk > 1 else pieces[0], starts

    ceq, _ = cumsum(jnp.where(eq, 1.0, 0.0))
    sel = gt | (eq & (ceq <= need))
    csel, starts = cumsum(jnp.where(sel, 1.0, 0.0))
    pos_ref[0] = jnp.where(sel, csel.astype(jnp.int32) - 1, -1)
    lane = lax.broadcasted_iota(jnp.int32, (ne, LANES), 1)
    offs = jnp.zeros((ne, LANES), F32)
    for j, s in enumerate(starts):
        offs = jnp.where(lane == j, s, offs)
    offs_ref[0] = offs.astype(jnp.int32)


def select_call(afft, cap, tile):
    b, ne, t = afft.shape
    assert t % tile == 0 and t // tile + 1 <= LANES
    tri = jnp.asarray(np.triu(np.ones((tile, tile), np.float32)), BF16)
    pos, offs = pl.pallas_call(
        functools.partial(_select_body, cap=cap, tile=tile),
        grid=(b,),
        in_specs=[pl.BlockSpec((1, ne, t), lambda i: (i, 0, 0)),
                  pl.BlockSpec((tile, tile), lambda i: (0, 0))],
        out_specs=[pl.BlockSpec((1, ne, t), lambda i: (i, 0, 0)),
                   pl.BlockSpec((1, ne, LANES), lambda i: (i, 0, 0))],
        out_shape=[jax.ShapeDtypeStruct((b, ne, t), jnp.int32),
                   jax.ShapeDtypeStruct((b, ne, LANES), jnp.int32)],
        compiler_params=_cparams(("arbitrary",)),
        name="select",
    )(afft, tri)
    nt = t // tile
    offs_flat = jnp.transpose(offs[:, :, :nt + 1], (0, 2, 1)).reshape(-1)
    return pos, offs_flat


def _align_down(v):
    return lax.shift_left(lax.shift_right_logical(v, 4), 4)


def _tile_slots(offs_ref, b, j, e, nt, ne):
    s0 = offs_ref[(b * (nt + 1) + j) * ne + e]
    s1 = offs_ref[(b * (nt + 1) + j + 1) * ne + e]
    return s0, s1


def _one_hot_t(pos_row, base, win, value=None):
    tt = pos_row.shape[1]
    k = lax.broadcasted_iota(jnp.int32, (win, tt), 0)
    hit = k == (pos_row - base)
    if value is None:
        return jnp.where(hit, 1.0, 0.0).astype(BF16)
    return jnp.where(hit, value, 0.0).astype(BF16)


def _dispatch_body(offs_ref, hx_ref, pos_ref, x_hbm, stage, stage2, carry, sem, sem2, *, nt, ne, win, cap):
    i = pl.program_id(0)
    n_steps = pl.num_programs(0)
    b, j = i // nt, i % nt
    slot = i % 2
    d = hx_ref.shape[2]

    @pl.when(j == 0)
    def _():
        carry[...] = jnp.zeros(carry.shape, carry.dtype)

    hx = hx_ref[0]
    s0s, s1s, a0s = [], [], []
    for e in range(ne):
        s0, s1 = _tile_slots(offs_ref, b, j, e, nt, ne)
        s0s.append(s0)
        s1s.append(s1)
        a0s.append(_align_down(s0))

    def windows(c):
        q = jnp.concatenate([_one_hot_t(pos_ref[0, e:e + 1, :], a0s[e] + c * win, win) for e in range(ne)], axis=0)
        return jnp.dot(q, hx, preferred_element_type=F32)

    def chunk_copy(buf, e, c, semref):
        dst = x_hbm.at[e, b, pl.ds(pl.multiple_of(a0s[e] + c * win, BF16_ROWS), win), :]
        return pltpu.make_async_copy(buf.at[e], dst, semref.at[e])

    def first_chunk_copy(step, e):
        bb, jj = step // nt, step % nt
        a = _align_down(offs_ref[(bb * (nt + 1) + jj) * ne + e])
        sl = step % 2
        dst = x_hbm.at[e, bb, pl.ds(pl.multiple_of(a, BF16_ROWS), win), :]
        return pltpu.make_async_copy(stage.at[sl, e], dst, sem.at[sl, e])

    xw = windows(0).reshape(ne, win, d)
    head = xw[:, :BF16_ROWS, :] + carry[...].astype(F32)
    stage[slot, :, :BF16_ROWS, :] = head.astype(BF16)
    stage[slot, :, BF16_ROWS:, :] = xw[:, BF16_ROWS:, :].astype(BF16)

    @pl.when(i > 0)
    def _():
        for e in range(ne):
            first_chunk_copy(i - 1, e).wait()

    for e in range(ne):
        first_chunk_copy(i, e).start()
        off = _align_down(s1s[e]) - a0s[e]
        rows = stage[slot, e, pl.ds(pl.multiple_of(jnp.minimum(off, win - BF16_ROWS), BF16_ROWS), BF16_ROWS), :]
        carry[e] = jnp.where(off < win, rows, carry[e])

    nch = [lax.div(s1s[e] - a0s[e], win) + 1 for e in range(ne)]
    nmax = functools.reduce(jnp.maximum, nch)

    def extra(c, cr):
        xc = windows(c).reshape(ne, win, d)
        stage2[...] = xc.astype(BF16)
        for e in range(ne):
            @pl.when(c < nch[e])
            def _():
                cp = chunk_copy(stage2, e, c, sem2)
                cp.start()
                cp.wait()
                off = _align_down(s1s[e]) - a0s[e] - c * win

                @pl.when(jnp.logical_and(off >= 0, off < win))
                def _():
                    carry[e] = stage2[e, pl.ds(pl.multiple_of(off, BF16_ROWS), BF16_ROWS), :]
        return cr

    lax.fori_loop(1, nmax, extra, 0)

    if nt > 1:
        assert cap >= win

        @pl.when(j == 0)
        def _():
            stage2[0] = jnp.zeros((win, d), BF16)
            tails = [pltpu.make_async_copy(stage2.at[0], x_hbm.at[e, b, pl.ds(cap, win), :], sem2.at[e])
                     for e in range(ne)]
            for cp in tails:
                cp.start()
            for cp in tails:
                cp.wait()

    @pl.when(i == n_steps - 1)
    def _():
        for e in range(ne):
            first_chunk_copy(i, e).wait()


def dispatch_call(hx, pos, offs_flat, cap, tile, win):
    b, t, d = hx.shape
    ne = pos.shape[1]
    nt = t // tile
    cpad = cap + win if nt > 1 else win
    assert nt > 1 or cap < win
    gs = pltpu.PrefetchScalarGridSpec(
        num_scalar_prefetch=1,
        grid=(b * nt,),
        in_specs=[pl.BlockSpec((1, tile, d), lambda i, o: (i // nt, i % nt, 0)),
                  pl.BlockSpec((1, ne, tile), lambda i, o: (i // nt, 0, i % nt))],
        out_specs=pl.BlockSpec(memory_space=pl.ANY),
        scratch_shapes=[pltpu.VMEM((2, ne, win, d), BF16), pltpu.VMEM((ne, win, d), BF16),
                        pltpu.VMEM((ne, BF16_ROWS, d), BF16),
                        pltpu.SemaphoreType.DMA((2, ne)), pltpu.SemaphoreType.DMA((ne,))])
    return pl.pallas_call(
        functools.partial(_dispatch_body, nt=nt, ne=ne, win=win, cap=cap),
        grid_spec=gs,
        out_shape=jax.ShapeDtypeStruct((ne, b, cpad, d), BF16),
        compiler_params=_cparams(("arbitrary",)),
        name="dispatch",
    )(offs_flat, hx, pos)


def _ffn_body(x_ref, wg_ref, wu_ref, wd_ref, y_ref, wgb, wub, wdb, *, rows, rb, batched):
    def cast_weights():
        for src, dst in ((wg_ref, wgb), (wu_ref, wub), (wd_ref, wdb)):
            for r in range(0, src.shape[0], 256):
                dst[r:r + 256, :] = src[r:r + 256, :].astype(BF16)

    if batched:
        pl.when(pl.program_id(1) == 0)(cast_weights)
    else:
        cast_weights()
    for r in range(0, rows, rb):
        xb = x_ref[r:r + rb, :]
        a = jnp.dot(xb, wgb[...], preferred_element_type=F32)
        g = jnp.dot(xb, wub[...], preferred_element_type=F32)
        h = (jax.nn.silu(a) * g).astype(BF16)
        y_ref[r:r + rb, :] = jnp.dot(h, wdb[...], preferred_element_type=F32).astype(BF16)
    if y_ref.shape[0] > rows:
        y_ref[rows:, :] = jnp.zeros((y_ref.shape[0] - rows, y_ref.shape[1]), BF16)


def ffn_call(xs, wg, wu, wd, layer, cap):
    ne, b, cpad, d = xs.shape
    f = wg.shape[3]
    if cap % 256 == 0:
        grid = (ne, b)
        x_spec = pl.BlockSpec((None, None, cap, d), lambda e, i: (e, i, 0, 0))
        y_spec = pl.BlockSpec((None, None, cpad, d), lambda e, i: (e, i, 0, 0))
        wmap = lambda e, i: (layer, e, 0, 0)
        body = functools.partial(_ffn_body, rows=cap, rb=256, batched=True)
        sem = ("arbitrary", "arbitrary")
        args, shape = (xs, wg, wu, wd), xs.shape
    else:
        rows = b * cpad
        grid = (ne,)
        x_spec = pl.BlockSpec((None, rows, d), lambda e: (e, 0, 0))
        y_spec = pl.BlockSpec((None, rows, d), lambda e: (e, 0, 0))
        wmap = lambda e: (layer, e, 0, 0)
        body = functools.partial(_ffn_body, rows=rows, rb=rows, batched=False)
        sem = ("arbitrary",)
        args, shape = (xs.reshape(ne, rows, d), wg, wu, wd), (ne, rows, d)
    y = pl.pallas_call(
        body,
        grid=grid,
        in_specs=[x_spec, pl.BlockSpec((None, None, d, f), wmap), pl.BlockSpec((None, None, d, f), wmap),
                  pl.BlockSpec((None, None, f, d), wmap)],
        out_specs=y_spec,
        out_shape=jax.ShapeDtypeStruct(shape, BF16),
        scratch_shapes=[pltpu.VMEM((d, f), BF16), pltpu.VMEM((d, f), BF16), pltpu.VMEM((f, d), BF16)],
        compiler_params=_cparams(sem),
        name="ffn",
    )(*args)
    return y.reshape(xs.shape)


def _combine_body(offs_ref, x1_ref, pos_ref, aff_ref, mod_ref, g_ref, y_hbm, o_ref, ywin, ywin2, sem, sem2,
                  *, nt, ne, win, nsub):
    i = pl.program_id(0)
    n_steps = pl.num_programs(0)
    spb = nt // nsub
    b = i // spb
    slot = i % 2
    tile = x1_ref.shape[1] // nsub
    d = x1_ref.shape[2]
    cpad = y_hbm.shape[2]
    tn_dims = (((0,), (0,)), ((), ()))

    def window_copy(step, u, e, sl):
        bb, jj = step // spb, (step % spb) * nsub + u
        a = _align_down(offs_ref[(bb * (nt + 1) + jj) * ne + e])
        return pltpu.make_async_copy(y_hbm.at[e, bb, pl.ds(pl.multiple_of(a, BF16_ROWS), win), :],
                                     ywin.at[sl, u, e], sem.at[sl, u, e])

    @pl.when(i == 0)
    def _():
        for u in range(nsub):
            for e in range(ne):
                window_copy(i, u, e, slot).start()

    @pl.when(i + 1 < n_steps)
    def _():
        for u in range(nsub):
            for e in range(ne):
                window_copy(i + 1, u, e, 1 - slot).start()

    gg = g_ref[...] * mod_ref[0][5:6]
    for u in range(nsub):
        for e in range(ne):
            window_copy(i, u, e, slot).wait()
    slow_paths = []
    for u in range(nsub):
        j = (i % spb) * nsub + u
        cols = slice(u * tile, (u + 1) * tile)
        s1s, a0s = [], []
        for e in range(ne):
            s0, s1 = _tile_slots(offs_ref, b, j, e, nt, ne)
            s1s.append(s1)
            a0s.append(_align_down(s0))

        def weights(c, a0s=a0s, cols=cols):
            return jnp.concatenate(
                [_one_hot_t(pos_ref[0, e:e + 1, cols], a0s[e] + c * win, win, aff_ref[0, e:e + 1, cols])
                 for e in range(ne)], axis=0)

        def first_chunk(u=u, weights=weights):
            return lax.dot_general(weights(0), ywin[slot, u].reshape(ne * win, d), tn_dims,
                                   preferred_element_type=F32)

        def finish(moe, cols=cols):
            o_ref[0, cols, :] = x1_ref[0, cols, :] + _rms(moe, gg)

        finish(first_chunk())

        nch = [lax.div(s1s[e] - a0s[e], win) + 1 for e in range(ne)]
        nmax = functools.reduce(jnp.maximum, nch)

        def slow(a0s=a0s, weights=weights, first_chunk=first_chunk, finish=finish, nmax=nmax):
            def extra(c, acc):
                for e in range(ne):
                    a = jnp.minimum(a0s[e] + c * win, cpad - win)
                    cp = pltpu.make_async_copy(y_hbm.at[e, b, pl.ds(pl.multiple_of(a, BF16_ROWS), win), :],
                                               ywin2.at[e], sem2.at[e])
                    cp.start()
                    cp.wait()
                return acc + lax.dot_general(weights(c), ywin2[...].reshape(ne * win, d), tn_dims,
                                             preferred_element_type=F32)

            finish(lax.fori_loop(1, nmax, extra, first_chunk()))

        slow_paths.append((nmax, slow))

    for nmax, slow in slow_paths:
        pl.when(nmax > 1)(slow)


def combine_call(x1, ys, pos, afft, offs_flat, mod3, mod_row, g_post, tile, win):
    b, t, d = x1.shape
    ne = pos.shape[1]
    nt = t // tile
    nsub = 2 if nt % 2 == 0 else 1
    spb = nt // nsub
    tt = nsub * tile
    mod_idx = (lambda i, o: (i // spb, 0, 0)) if mod_row is None else (lambda i, o: (mod_row, 0, 0))
    gs = pltpu.PrefetchScalarGridSpec(
        num_scalar_prefetch=1,
        grid=(b * spb,),
        in_specs=[pl.BlockSpec((1, tt, d), lambda i, o: (i // spb, i % spb, 0)),
                  pl.BlockSpec((1, ne, tt), lambda i, o: (i // spb, 0, i % spb)),
                  pl.BlockSpec((1, ne, tt), lambda i, o: (i // spb, 0, i % spb)),
                  pl.BlockSpec((1, 6, d), mod_idx),
                  pl.BlockSpec((1, d), lambda i, o: (0, 0)),
                  pl.BlockSpec(memory_space=pl.ANY)],
        out_specs=pl.BlockSpec((1, tt, d), lambda i, o: (i // spb, i % spb, 0)),
        scratch_shapes=[pltpu.VMEM((2, nsub, ne, win, d), BF16), pltpu.VMEM((ne, win, d), BF16),
                        pltpu.SemaphoreType.DMA((2, nsub, ne)), pltpu.SemaphoreType.DMA((ne,))])
    return pl.pallas_call(
        functools.partial(_combine_body, nt=nt, ne=ne, win=win, nsub=nsub),
        grid_spec=gs,
        out_shape=jax.ShapeDtypeStruct((b, t, d), F32),
        compiler_params=_cparams(("arbitrary",)),
        name="combine",
    )(offs_flat, x1, pos, afft, mod3, g_post.reshape(1, d), ys)


def ec_moe_call(x1, hx, afft, mod3, mod_row, g_post, wg, wu, wd, layer):
    b, t, d = x1.shape
    ne = afft.shape[1]
    cap = EC_CAPACITY_FACTOR * t // ne
    tile = min(256, t)
    win = min(cap, 3 * EC_CAPACITY_FACTOR * tile // (2 * ne)) + BF16_ROWS
    win = -(-win // BF16_ROWS) * BF16_ROWS
    pos, offs_flat = select_call(afft, cap, tile)
    xs = dispatch_call(hx, pos, offs_flat, cap, tile, win)
    ys = ffn_call(xs, wg, wu, wd, layer, cap)
    return combine_call(x1, ys, pos, afft, offs_flat, mod3, mod_row, g_post, tile, win)


def kernel(x, c, ctx, c_ctx, w_mod, b_mod, g_mix_pre, g_mix_post, g_ffn_pre, g_ffn_post, w_in, s5_lam_re, s5_lam_im,
           s5_log_dt, s5_b_re, s5_b_im, s5_c_re, s5_c_im, s5_d, s5_w_glu, pool_w, pool_scale, w_out, w_router,
           w_gate, w_up, w_down):
    depth = w_mod.shape[0]
    b, t, d = x.shape
    _, _, g, p, h = s5_b_re.shape
    ds = g * h
    dp = d - ds
    assert t % min(512, t) == 0 and t % LANES == 0 and t % GRID_W == 0
    assert ctx.shape[1] % LANES == 0 and 2 * p == LANES
    nmod = w_mod.shape[2] // d
    rows = -(-(b + 1) // SUBLANES) * SUBLANES
    cc = jnp.zeros((rows, d), F32).at[:b].set(c.astype(F32)).at[b].set(c_ctx.astype(F32))
    ncl, ncc = t // LANES, ctx.shape[1] // LANES

    for l in range(depth):
        last = l == depth - 1
        mod3 = mod_call(cc, w_mod[l].astype(F32), b_mod[l].astype(F32)).reshape(rows, nmod, d)
        ws_t = jnp.transpose(w_in[l][:, :ds]).astype(BF16)
        wp = w_in[l][:, ds:].astype(BF16)
        wtoe, ptab, htab, dr, di = s5_tables(s5_lam_re[l], s5_lam_im[l], s5_log_dt[l], s5_b_re[l], s5_b_im[l],
                                             s5_c_re[l], s5_c_im[l], s5_d[l])
        pw = pool_w[l].astype(BF16)
        ps = pool_scale[l].astype(F32).reshape(1, dp)
        wglu_t = jnp.transpose(s5_w_glu[l]).astype(BF16)
        wo_s, wo_p = w_out[l][:ds].astype(BF16), w_out[l][ds:].astype(BF16)
        wr2 = router_split(w_router[l])
        ne = w_router.shape[2]
        wg, wu, wd = w_gate.astype(F32), w_up.astype(F32), w_down.astype(F32)

        pxt, pxp = mixer_in_call(x, mod3, None, g_mix_pre[l], ws_t, wp)
        pct, pcp = mixer_in_call(ctx, mod3, b, g_mix_pre[l], ws_t, wp)
        xl, xc = pxt.reshape(ncl * b, g, h, LANES), pct.reshape(ncc * b, g, h, LANES)
        sl, sc = s5_states_call(xl, xc, ptab, htab)
        hl, hc = s5_recur_call(sl.reshape(N_HALF, ncl, b, -1), sc.reshape(N_HALF, ncc, b, -1), dr, di, p)
        hl, hc = hl.reshape(N_HALF, ncl * b, -1), hc.reshape(N_HALF, ncc * b, -1)
        if last:
            (yt,) = s5_main_call([xl], [hl], wtoe, ptab, htab)
        else:
            yt, yc = s5_main_call([xl, xc], [hl, hc], wtoe, ptab, htab)
        pool_x = pool_grid_call(pxp, pw, ps)
        x1, hx, afft = mixer_out_call(x, yt.reshape(ncl, b, ds, LANES), pool_x, mod3, None, wglu_t, wo_s, wo_p,
                                      g_mix_post[l], g_ffn_pre[l], wr2, ne)
        x = ec_moe_call(x1, hx, afft, mod3, None, g_ffn_post[l], wg, wu, wd, l)
        if not last:
            pool_c = pool_seq_call(pcp, pw, ps)
            c1, hcx, affc = mixer_out_call(ctx, yc.reshape(ncc, b, ds, LANES), pool_c, mod3, b, wglu_t, wo_s, wo_p,
                                           g_mix_post[l], g_ffn_pre[l], wr2, ne)
            ctx = ec_moe_call(c1, hcx, affc, mod3, b, g_ffn_post[l], wg, wu, wd, l)
    return x
```
